```python
import jax, jax.numpy as jnp
from jax import lax
import numpy as np

D_MODEL = 2048
BATCH = 4
SEQ = 2048
DEPTH = 4

N_HEADS = 16
N_KV_HEADS = 2
HEAD_DIM = 64
GROUP = N_HEADS // N_KV_HEADS
ATTN_WIDTH = N_HEADS * HEAD_DIM
KV_WIDTH = N_KV_HEADS * HEAD_DIM
WINDOW = 128
BLOCK = 128
LRU_WIDTH = 1024
LRU_BLOCKS = 8
LRU_BLOCK_W = LRU_WIDTH // LRU_BLOCKS
CONV_WIDTH = 4
LRU_C = 8.0
D_FF = 5632
FFN_RES = 0.5
N_BRANCH = 2
N_MOD = 9
EPS = 1e-6
SPLITS = tuple(np.cumsum([LRU_WIDTH, LRU_WIDTH, ATTN_WIDTH, KV_WIDTH, KV_WIDTH, D_MODEL])[:].tolist())
IN_COLS = 2 * LRU_WIDTH + ATTN_WIDTH + 2 * KV_WIDTH + N_BRANCH * D_MODEL

kernel_name = "hybrid_rglru_swa_sink_macaron_adaln"


def rmsnorm(x, g):
    xf = x.astype(jnp.float32)
    y = xf * lax.rsqrt(jnp.mean(xf * xf, axis=-1, keepdims=True) + EPS) * g.astype(jnp.float32)
    return y.astype(x.dtype)


def modulate(h, shift, scale):
    return h * (1 + scale[:, None, :]) + shift[:, None, :]


def swiglu(h, w_up, w_down):
    gu = h @ w_up
    g, u = jnp.split(gu, 2, axis=-1)
    return (jax.nn.silu(g) * u) @ w_down


def causal_depthwise_conv(x, w, b):
    S = x.shape[1]
    xp = jnp.pad(x, ((0, 0), (CONV_WIDTH - 1, 0), (0, 0)))
    y = b
    for k in range(CONV_WIDTH):
        y = y + xp[:, k:k + S, :] * w[k]
    return y


def block_diag_linear(x, w, b):
    B, S, _ = x.shape
    xb = x.reshape(B, S, LRU_BLOCKS, LRU_BLOCK_W)
    y = jnp.einsum('bsnd,nde->bsne', xb, w).reshape(B, S, LRU_WIDTH)
    return y + b


def _lru_combine(earlier, later):
    a1, b1 = earlier
    a2, b2 = later
    return a1 * a2, a2 * b1 + b2


def rg_lru(x, w_a, b_a, w_x, b_x, lam):
    r = jax.nn.sigmoid(block_diag_linear(x, w_a, b_a)).astype(jnp.float32)
    i = jax.nn.sigmoid(block_diag_linear(x, w_x, b_x)).astype(jnp.float32)
    log_a = -LRU_C * r * jax.nn.softplus(-lam.astype(jnp.float32))
    a = jnp.exp(log_a)
    mult = jnp.sqrt(-jnp.expm1(2.0 * log_a))
    b = mult * (i * x.astype(jnp.float32))
    _, h = lax.associative_scan(_lru_combine, (a, b), axis=1)
    return h.astype(x.dtype)


def sliding_window_attention(q, k, v, sinks):
    B, S, _ = q.shape
    nb = S // BLOCK
    qb = q.reshape(B, nb, BLOCK, N_KV_HEADS, GROUP, HEAD_DIM)
    kb = k.reshape(B, nb, BLOCK, N_KV_HEADS, HEAD_DIM)
    vb = v.reshape(B, nb, BLOCK, N_KV_HEADS, HEAD_DIM)
    pad = ((0, 0), (1, 0), (0, 0), (0, 0), (0, 0))
    kk = jnp.concatenate([jnp.pad(kb, pad)[:, :-1], kb], axis=2)
    vv = jnp.concatenate([jnp.pad(vb, pad)[:, :-1], vb], axis=2)
    scores = jnp.einsum('bnqhgd,bnkhd->bnhgqk', qb, kk).astype(jnp.float32) * (HEAD_DIM ** -0.5)
    q_pos = jnp.arange(BLOCK)[:, None] + BLOCK
    k_pos = jnp.arange(2 * BLOCK)[None, :]
    diff = q_pos - k_pos
    band = (diff >= 0) & (diff < WINDOW)
    valid = (jnp.arange(nb)[:, None] > 0) | (k_pos >= BLOCK)
    mask = band[None, :, :] & valid[:, None, :]
    scores = jnp.where(mask[None, :, None, None, :, :], scores, -jnp.inf)
    sink = jnp.broadcast_to(sinks.astype(jnp.float32).reshape(N_KV_HEADS, GROUP)[None, None, :, :, None, None],
                            scores.shape[:-1] + (1,))
    p = jax.nn.softmax(jnp.concatenate([scores, sink], axis=-1), axis=-1)[..., :-1]
    out = jnp.einsum('bnhgqk,bnkhd->bnqhgd', p.astype(v.dtype), vv)
    return out.reshape(B, S, ATTN_WIDTH)


def setup_inputs(seed: int = 0) -> dict:
    key = jax.random.key(seed)
    ks = jax.random.split(key, 26)

    def nrm(k, shape, scale):
        return jax.random.normal(k, shape, jnp.float32) * scale

    a_init = jax.random.uniform(ks[14], (DEPTH, LRU_WIDTH), jnp.float32, minval=0.9, maxval=0.999)
    return {
        "x": nrm(ks[0], (BATCH, SEQ, D_MODEL), 1.0),
        "c": nrm(ks[1], (BATCH, D_MODEL), 1.0),
        "g_ffn1": 1.0 + nrm(ks[2], (DEPTH, D_MODEL), 0.02),
        "w_ffn1_up": nrm(ks[3], (DEPTH, D_MODEL, 2 * D_FF), D_MODEL ** -0.5),
        "w_ffn1_down": nrm(ks[4], (DEPTH, D_FF, D_MODEL), D_FF ** -0.5),
        "g_mix": 1.0 + nrm(ks[5], (DEPTH, D_MODEL), 0.02),
        "w_in": nrm(ks[6], (DEPTH, D_MODEL, IN_COLS), D_MODEL ** -0.5),
        "conv_w": nrm(ks[7], (DEPTH, CONV_WIDTH, LRU_WIDTH), CONV_WIDTH ** -0.5),
        "conv_b": nrm(ks[8], (DEPTH, LRU_WIDTH), 0.01),
        "lru_wa": nrm(ks[9], (DEPTH, LRU_BLOCKS, LRU_BLOCK_W, LRU_BLOCK_W), LRU_BLOCK_W ** -0.5),
        "lru_ba": nrm(ks[10], (DEPTH, LRU_WIDTH), 0.01),
        "lru_wx": nrm(ks[11], (DEPTH, LRU_BLOCKS, LRU_BLOCK_W, LRU_BLOCK_W), LRU_BLOCK_W ** -0.5),
        "lru_bx": nrm(ks[12], (DEPTH, LRU_WIDTH), 0.01),
        "lru_lambda": jnp.log(a_init) - jnp.log1p(-a_init),
        "attn_sinks": nrm(ks[13], (DEPTH, N_HEADS), 0.5),
        "w_br_rnn": nrm(ks[15], (DEPTH, LRU_WIDTH, D_MODEL), LRU_WIDTH ** -0.5),
        "w_br_attn": nrm(ks[16], (DEPTH, ATTN_WIDTH, D_MODEL), ATTN_WIDTH ** -0.5),
        "w_out": nrm(ks[17], (DEPTH, D_MODEL, D_MODEL), D_MODEL ** -0.5),
        "g_ffn2": 1.0 + nrm(ks[18], (DEPTH, D_MODEL), 0.02),
        "w_ffn2_up": nrm(ks[19], (DEPTH, D_MODEL, 2 * D_FF), D_MODEL ** -0.5),
        "w_ffn2_down": nrm(ks[20], (DEPTH, D_FF, D_MODEL), D_FF ** -0.5),
        "w_mod": nrm(ks[21], (DEPTH, D_MODEL, N_MOD * D_MODEL), 0.5 * D_MODEL ** -0.5),
        "b_mod": nrm(ks[22], (DEPTH, N_MOD * D_MODEL), 0.01),
        "g_final": 1.0 + nrm(ks[23], (D_MODEL,), 0.02),
    }


def reference(x, c, g_ffn1, w_ffn1_up, w_ffn1_down, g_mix, w_in, conv_w, conv_b,
              lru_wa, lru_ba, lru_wx, lru_bx, lru_lambda, attn_sinks,
              w_br_rnn, w_br_attn, w_out, g_ffn2, w_ffn2_up, w_ffn2_down,
              w_mod, b_mod, g_final):
    c_act = jax.nn.silu(c)
    for l in range(DEPTH):
        mod = c_act @ w_mod[l] + b_mod[l]
        sh1, sc1, ga1, sh2, sc2, ga2, sh3, sc3, ga3 = jnp.split(mod, N_MOD, axis=-1)

        h = modulate(rmsnorm(x, g_ffn1[l]), sh1, sc1)
        x = x + FFN_RES * ga1[:, None, :] * swiglu(h, w_ffn1_up[l], w_ffn1_down[l])

        h = modulate(rmsnorm(x, g_mix[l]), sh2, sc2)
        z = h @ w_in[l]
        xr, gr, q, k, v, gate_a, gate_b = jnp.split(z, SPLITS, axis=-1)
        u = causal_depthwise_conv(xr, conv_w[l], conv_b[l])
        rnn = jax.nn.gelu(gr) * rg_lru(u, lru_wa[l], lru_ba[l], lru_wx[l], lru_bx[l], lru_lambda[l])
        attn = sliding_window_attention(q, k, v, attn_sinks[l])
        merged = jax.nn.sigmoid(gate_a) * (rnn @ w_br_rnn[l]) + jax.nn.sigmoid(gate_b) * (attn @ w_br_attn[l])
        x = x + ga2[:, None, :] * (merged @ w_out[l])

        h = modulate(rmsnorm(x, g_ffn2[l]), sh3, sc3)
        x = x + FFN_RES * ga3[:, None, :] * swiglu(h, w_ffn2_up[l], w_ffn2_down[l])
    return rmsnorm(x, g_final)
```

```python
import functools

import jax
import jax.numpy as jnp
from jax import lax
from jax.experimental import pallas as pl
from jax.experimental.pallas import tpu as pltpu

F32 = jnp.float32
BF16 = jnp.bfloat16

N_HEADS = 16
N_KV_HEADS = 2
HEAD_DIM = 64
WINDOW = 128
LRU_C = 8.0
FFN_RES = 0.5
EPS = 1e-6
N_MOD = 9
SOFTMAX_SCALE = HEAD_DIM ** -0.5
MASK_VALUE = -1e30

VMEM_LIMIT_BYTES = 56 * 1024 * 1024
SUBLANES = 8
LANES = 128

MOD_TN = 1024
FFN_TM = 512
FFN_TF = 512
INPROJ_TM = 1024
INPROJ_TN = 1024
LRU_TC = 256
ATTN_TQ = 512
MERGE_TM = 256


def _params(*sem):
    return pltpu.CompilerParams(dimension_semantics=sem, vmem_limit_bytes=VMEM_LIMIT_BYTES)


def _sigmoid(x):
    return 1.0 / (1.0 + jnp.exp(-x))


def _gelu_tanh(x):
    c = 0.7978845608028654
    return 0.5 * x * (1.0 + jnp.tanh(c * (x + 0.044715 * (x * x * x))))


def _rmsnorm(x, g):
    ms = jnp.mean(x * x, axis=-1, keepdims=True)
    return x * lax.rsqrt(ms + EPS) * g


def _normmod(x, g, shift, scale):
    return _rmsnorm(x, g) * (1.0 + scale) + shift


def _mod_kernel(c_ref, w_ref, b_ref, o_ref):
    c = c_ref[...]
    ca = (c * _sigmoid(c)).astype(BF16)
    o_ref[...] = jnp.dot(ca, w_ref[...].astype(BF16), preferred_element_type=F32) + b_ref[...]


def _modulation(c_pad, w_mod, b_mod):
    depth, d, nd = w_mod.shape
    rows = c_pad.shape[0]
    return pl.pallas_call(
        _mod_kernel,
        grid=(depth, nd // MOD_TN),
        in_specs=[
            pl.BlockSpec((rows, d), lambda l, j: (0, 0)),
            pl.BlockSpec((None, d, MOD_TN), lambda l, j: (l, 0, j)),
            pl.BlockSpec((None, 1, MOD_TN), lambda l, j: (l, 0, j)),
        ],
        out_specs=pl.BlockSpec((None, rows, MOD_TN), lambda l, j: (l, 0, j)),
        out_shape=jax.ShapeDtypeStruct((depth, rows, nd), F32),
        compiler_params=_params("arbitrary", "arbitrary"),
        name="modulation",
    )(c_pad, w_mod, b_mod.reshape(depth, 1, nd))


def _mod_spec(d, tiles_per_batch, chunk):
    return pl.BlockSpec((None, 1, d), lambda i, j: (i // tiles_per_batch, 0, chunk))


def _ffn_kernel(x_ref, g_ref, sh_ref, sc_ref, ga_ref, wg_ref, wu_ref, wd_ref, gf_ref, o_ref, h_ref,
                *, final_norm):
    f = pl.program_id(1)

    @pl.when(f == 0)
    def _():
        x = x_ref[...]
        h_ref[...] = _normmod(x, g_ref[...], sh_ref[...], sc_ref[...]).astype(BF16)
        o_ref[...] = x

    h = h_ref[...]
    gate = jnp.dot(h, wg_ref[...], preferred_element_type=F32)
    up = jnp.dot(h, wu_ref[...], preferred_element_type=F32)
    act = (gate * _sigmoid(gate) * up).astype(BF16)
    coef = FFN_RES * ga_ref[...]
    o_ref[...] += coef * jnp.dot(act, wd_ref[...], preferred_element_type=F32)

    if final_norm:
        @pl.when(f == pl.num_programs(1) - 1)
        def _():
            o_ref[...] = _rmsnorm(o_ref[...], gf_ref[...])


def _ffn(x, g, mod_l, chunk0, w_up, w_down, layer, seq, g_final, final_norm):
    tok, d = x.shape
    dff = w_down.shape[1]
    nf = dff // FFN_TF
    tpb = seq // FFN_TM
    kern = functools.partial(_ffn_kernel, final_norm=final_norm)
    return pl.pallas_call(
        kern,
        grid=(tok // FFN_TM, nf),
        in_specs=[
            pl.BlockSpec((FFN_TM, d), lambda i, f: (i, 0)),
            pl.BlockSpec((None, 1, d), lambda i, f: (layer, 0, 0)),
            _mod_spec(d, tpb, chunk0),
            _mod_spec(d, tpb, chunk0 + 1),
            _mod_spec(d, tpb, chunk0 + 2),
            pl.BlockSpec((None, d, FFN_TF), lambda i, f: (layer, 0, f)),
            pl.BlockSpec((None, d, FFN_TF), lambda i, f: (layer, 0, nf + f)),
            pl.BlockSpec((None, FFN_TF, d), lambda i, f: (layer, f, 0)),
            pl.BlockSpec((1, d), lambda i, f: (0, 0)),
        ],
        out_specs=pl.BlockSpec((FFN_TM, d), lambda i, f: (i, 0)),
        out_shape=jax.ShapeDtypeStruct((tok, d), F32),
        scratch_shapes=[pltpu.VMEM((FFN_TM, d), BF16)],
        compiler_params=_params("arbitrary", "arbitrary"),
        name="ffn",
    )(x, g, mod_l, mod_l, mod_l, w_up, w_up, w_down, g_final)


INPROJ_SIGMOID_BLOCKS = 4
INPROJ_GELU_BLOCK = 5


def _inproj_kernel(x_ref, g_ref, sh_ref, sc_ref, w_ref, o_ref, h_ref, *, n_full, tail):
    j = pl.program_id(1)

    @pl.when(j == 0)
    def _():
        h_ref[...] = _normmod(x_ref[...], g_ref[...], sh_ref[...], sc_ref[...]).astype(BF16)

    h = h_ref[...]

    @pl.when(j < INPROJ_SIGMOID_BLOCKS)
    def _():
        o_ref[...] = _sigmoid(jnp.dot(h, w_ref[...], preferred_element_type=F32))

    @pl.when(j == INPROJ_GELU_BLOCK)
    def _():
        o_ref[...] = _gelu_tanh(jnp.dot(h, w_ref[...], preferred_element_type=F32))

    @pl.when((j >= INPROJ_SIGMOID_BLOCKS) & (j != INPROJ_GELU_BLOCK) & (j < n_full))
    def _():
        o_ref[...] = jnp.dot(h, w_ref[...], preferred_element_type=F32)

    @pl.when(j == n_full)
    def _():
        o_ref[:, :tail] = jnp.dot(h, w_ref[:, :tail], preferred_element_type=F32)


def _inproj(x, g, mod_l, w_in_r, layer, seq):
    tok, d = x.shape
    ncol = w_in_r.shape[2]
    n_full = ncol // INPROJ_TN
    tail = ncol - n_full * INPROJ_TN
    tpb = seq // INPROJ_TM
    kern = functools.partial(_inproj_kernel, n_full=n_full, tail=tail)
    return pl.pallas_call(
        kern,
        grid=(tok // INPROJ_TM, n_full + 1),
        in_specs=[
            pl.BlockSpec((INPROJ_TM, d), lambda i, j: (i, 0)),
            pl.BlockSpec((None, 1, d), lambda i, j: (layer, 0, 0)),
            _mod_spec(d, tpb, 3),
            _mod_spec(d, tpb, 4),
            pl.BlockSpec((None, d, INPROJ_TN), lambda i, j: (layer, 0, j)),
        ],
        out_specs=pl.BlockSpec((INPROJ_TM, INPROJ_TN), lambda i, j: (i, j)),
        out_shape=jax.ShapeDtypeStruct((tok, ncol), F32),
        scratch_shapes=[pltpu.VMEM((INPROJ_TM, d), BF16)],
        compiler_params=_params("arbitrary", "arbitrary"),
        name="inproj",
    )(x, g, mod_l, mod_l, w_in_r)


def _lru_kernel(xr_ref, gg_ref, cw_ref, cb_ref, wa_ref, ba_ref, wx_ref, bx_ref, lam_ref, o_ref,
                ext_ref, h_ref, a_ref, b_ref, hs_ref):
    tc, width = xr_ref.shape
    t = pl.program_id(1)

    @pl.when(t == 0)
    def _():
        ext_ref[0:SUBLANES, :] = jnp.zeros((SUBLANES, width), F32)
        h_ref[...] = jnp.zeros_like(h_ref)

    xr = xr_ref[...]
    ext_ref[SUBLANES:SUBLANES + tc, :] = xr
    kw = cw_ref.shape[0]
    u = cb_ref[...] + cw_ref[kw - 1:kw, :] * xr
    for k in range(kw - 1):
        back = kw - 1 - k
        u = u + cw_ref[k:k + 1, :] * ext_ref[SUBLANES - back:SUBLANES - back + tc, :]
    ext_ref[0:SUBLANES, :] = xr[tc - SUBLANES:tc, :]

    lam = lam_ref[...]
    nlam = -lam
    softplus = jnp.maximum(nlam, 0.0) + jnp.log1p(jnp.exp(-jnp.abs(nlam)))
    nb, bw, _ = wa_ref.shape
    for n in range(nb):
        sl = slice(n * bw, (n + 1) * bw)
        un = u[:, sl]
        unb = un.astype(BF16)
        r = _sigmoid(jnp.dot(unb, wa_ref[n], preferred_element_type=F32) + ba_ref[:, sl])
        ig = _sigmoid(jnp.dot(unb, wx_ref[n], preferred_element_type=F32) + bx_ref[:, sl])
        log_a = -LRU_C * r * softplus[:, sl]
        a = jnp.exp(log_a)
        a_ref[:, sl] = a
        b_ref[:, sl] = jnp.sqrt(-jnp.tanh(log_a) * (a * a + 1.0)) * (ig * un)

    row = lax.broadcasted_iota(jnp.int32, (SUBLANES, width), 0)

    def body(k, h):
        off = pl.multiple_of(k * SUBLANES, SUBLANES)
        a = a_ref[pl.ds(off, SUBLANES), :]
        b = b_ref[pl.ds(off, SUBLANES), :]
        for dist in (1, 2, 4):
            keep = row >= dist
            b = b + a * jnp.where(keep, pltpu.roll(b, dist, 0), 0.0)
            a = a * jnp.where(keep, pltpu.roll(a, dist, 0), 1.0)
        hs = a * h + b
        hs_ref[pl.ds(off, SUBLANES), :] = hs
        return hs[SUBLANES - 1:SUBLANES, :]

    h_ref[...] = lax.fori_loop(0, tc // SUBLANES, body, h_ref[...])
    o_ref[...] = (gg_ref[...] * hs_ref[...]).astype(o_ref.dtype)


def _lru(z, conv_w, conv_b, wa, ba, wx, bx, lam, layer, batch, seq, xr_block, gr_block):
    tok = z.shape[0]
    width = conv_w.shape[2]
    nt = seq // LRU_TC
    vec = lambda: pl.BlockSpec((None, 1, width), lambda b, t: (layer, 0, 0))
    blk = lambda: pl.BlockSpec((None,) + wa.shape[1:], lambda b, t: (layer, 0, 0, 0))
    return pl.pallas_call(
        _lru_kernel,
        grid=(batch, nt),
        in_specs=[
            pl.BlockSpec((LRU_TC, width), lambda b, t: (b * nt + t, xr_block)),
            pl.BlockSpec((LRU_TC, width), lambda b, t: (b * nt + t, gr_block)),
            pl.BlockSpec((None,) + conv_w.shape[1:], lambda b, t: (layer, 0, 0)),
            vec(), blk(), vec(), blk(), vec(), vec(),
        ],
        out_specs=pl.BlockSpec((LRU_TC, width), lambda b, t: (b * nt + t, 0)),
        out_shape=jax.ShapeDtypeStruct((tok, width), BF16),
        scratch_shapes=[
            pltpu.VMEM((LRU_TC + SUBLANES, width), F32),
            pltpu.VMEM((1, width), F32),
            pltpu.VMEM((LRU_TC, width), F32),
            pltpu.VMEM((LRU_TC, width), F32),
            pltpu.VMEM((LRU_TC, width), F32),
        ],
        compiler_params=_params("arbitrary", "arbitrary"),
        name="rglru",
    )(z, z, conv_w, conv_b, wa, ba, wx, bx, lam)


def _attn_kernel(sink_ref, q_ref, kvo_ref, kvp_ref, o_ref, *, layer):
    tq = q_ref.shape[0]
    first_valid_key = jnp.where(pl.program_id(1) == 0, WINDOW, 0)
    pairs = N_HEADS // N_KV_HEADS // 2
    rows = pairs * WINDOW
    lane = lax.broadcasted_iota(jnp.int32, (2 * WINDOW, LANES), 1)
    lo = lane < HEAD_DIM
    qpos = lax.broadcasted_iota(jnp.int32, (rows, 2 * WINDOW), 0) % WINDOW
    kpos = lax.broadcasted_iota(jnp.int32, (rows, 2 * WINDOW), 1)
    band = (kpos > qpos) & (kpos <= qpos + WINDOW)
    pair_of_row = lax.broadcasted_iota(jnp.int32, (rows, 1), 0) // WINDOW
    out_lo = lax.broadcasted_iota(jnp.int32, (rows, LANES), 1) < HEAD_DIM

    for jb in range(tq // WINDOW):
        own = kvo_ref[jb * WINDOW:(jb + 1) * WINDOW, :]
        prev = kvp_ref[...] if jb == 0 else kvo_ref[(jb - 1) * WINDOW:jb * WINDOW, :]
        kv = jnp.concatenate([prev, own], axis=0)
        kk, vv = kv[:, :LANES], kv[:, LANES:]
        kk_rot = pltpu.roll(kk, HEAD_DIM, 1)
        vv_rot = pltpu.roll(vv, HEAD_DIM, 1)
        if jb == 0:
            mask = band & (kpos >= first_valid_key)
        else:
            mask = band
        for kvh in range(N_KV_HEADS):
            if kvh == 0:
                k_top, k_bot, v_top, v_bot = kk, kk_rot, vv, vv_rot
            else:
                k_top, k_bot, v_top, v_bot = kk_rot, kk, vv_rot, vv
            kbd = jnp.concatenate([jnp.where(lo, k_top, 0.0), jnp.where(lo, 0.0, k_bot)], axis=0).astype(BF16)
            vbd = jnp.concatenate([jnp.where(lo, v_top, 0.0), jnp.where(lo, 0.0, v_bot)], axis=0).astype(BF16)
            qs = jnp.concatenate(
                [q_ref[jb * WINDOW:(jb + 1) * WINDOW, (pairs * kvh + p) * LANES:(pairs * kvh + p + 1) * LANES]
                 for p in range(pairs)], axis=0).astype(BF16)
            s = lax.dot_general(qs, kbd, (((1,), (1,)), ((), ())), preferred_element_type=F32)
            s = s * SOFTMAX_SCALE
            probs, inv = [], []
            for half in range(2):
                sh = jnp.where(mask, s[:, half * 2 * WINDOW:(half + 1) * 2 * WINDOW], MASK_VALUE)
                sink = jnp.zeros((rows, 1), F32)
                for p in range(pairs):
                    head = (N_HEADS // N_KV_HEADS) * kvh + 2 * p + half
                    sink = jnp.where(pair_of_row == p, sink_ref[layer, head], sink)
                m = jnp.maximum(jnp.max(sh, axis=-1, keepdims=True), sink)
                e = jnp.exp(sh - m)
                den = jnp.sum(e, axis=-1, keepdims=True) + jnp.exp(sink - m)
                probs.append(e.astype(BF16))
                inv.append(1.0 / den)
            pv = jnp.dot(jnp.concatenate(probs, axis=1), vbd, preferred_element_type=F32)
            pv = pv * jnp.where(out_lo, inv[0], inv[1])
            for p in range(pairs):
                col = (pairs * kvh + p) * LANES
                o_ref[jb * WINDOW:(jb + 1) * WINDOW, col:col + LANES] = (
                    pv[p * WINDOW:(p + 1) * WINDOW, :].astype(o_ref.dtype))


def _attention(z, sinks, layer, batch, seq, q_block, kv_block):
    tok = z.shape[0]
    width = N_HEADS * HEAD_DIM
    kvw = 2 * N_KV_HEADS * HEAD_DIM
    nt = seq // ATTN_TQ
    per = ATTN_TQ // WINDOW
    kern = functools.partial(_attn_kernel, layer=layer)
    return pl.pallas_call(
        kern,
        grid=(batch, nt),
        in_specs=[
            pl.BlockSpec(memory_space=pltpu.SMEM),
            pl.BlockSpec((ATTN_TQ, width), lambda b, i: (b * nt + i, q_block)),
            pl.BlockSpec((ATTN_TQ, kvw), lambda b, i: (b * nt + i, kv_block)),
            pl.BlockSpec((WINDOW, kvw), lambda b, i: (jnp.maximum((b * nt + i) * per - 1, 0), kv_block)),
        ],
        out_specs=pl.BlockSpec((ATTN_TQ, width), lambda b, i: (b * nt + i, 0)),
        out_shape=jax.ShapeDtypeStruct((tok, width), BF16),
        compiler_params=_params("arbitrary", "arbitrary"),
        name="swattn",
    )(sinks, z, z, z)


def _merge_kernel(x_ref, rnn_ref, attn_ref, sga_ref, sgb_ref, g2_ref, wr_ref, wa_ref, wo_ref, o_ref):
    br = jnp.dot(rnn_ref[...], wr_ref[...], preferred_element_type=F32)
    ba = jnp.dot(attn_ref[...], wa_ref[...], preferred_element_type=F32)
    merged = (sga_ref[...] * br + sgb_ref[...] * ba).astype(BF16)
    o_ref[...] = x_ref[...] + g2_ref[...] * jnp.dot(merged, wo_ref[...], preferred_element_type=F32)


def _merge(x, rnn, attn, z, mod_l, w_br_rnn, w_br_attn, w_out, layer, seq):
    tok, d = x.shape
    width = rnn.shape[1]
    tpb = seq // MERGE_TM
    resident = pl.Buffered(1)
    return pl.pallas_call(
        _merge_kernel,
        grid=(tok // MERGE_TM,),
        in_specs=[
            pl.BlockSpec((MERGE_TM, d), lambda i: (i, 0)),
            pl.BlockSpec((MERGE_TM, width), lambda i: (i, 0)),
            pl.BlockSpec((MERGE_TM, width), lambda i: (i, 0)),
            pl.BlockSpec((MERGE_TM, d), lambda i: (i, 0)),
            pl.BlockSpec((MERGE_TM, d), lambda i: (i, 1)),
            pl.BlockSpec((None, 1, d), lambda i: (i // tpb, 0, 5)),
            pl.BlockSpec((None, width, d), lambda i: (layer, 0, 0), pipeline_mode=resident),
            pl.BlockSpec((None, width, d), lambda i: (layer, 0, 0), pipeline_mode=resident),
            pl.BlockSpec((None, d, d), lambda i: (layer, 0, 0), pipeline_mode=resident),
        ],
        out_specs=pl.BlockSpec((MERGE_TM, d), lambda i: (i, 0)),
        out_shape=jax.ShapeDtypeStruct((tok, d), F32),
        compiler_params=_params("arbitrary"),
        name="merge",
    )(x, rnn, attn, z, z, mod_l, w_br_rnn, w_br_attn, w_out)


def kernel(x, c, g_ffn1, w_ffn1_up, w_ffn1_down, g_mix, w_in, conv_w, conv_b, lru_wa, lru_ba, lru_wx,
           lru_bx, lru_lambda, attn_sinks, w_br_rnn, w_br_attn, w_out, g_ffn2, w_ffn2_up, w_ffn2_down,
           w_mod, b_mod, g_final):
    batch, seq, d = x.shape
    depth = w_mod.shape[0]
    lru_w = conv_w.shape[2]
    attn_w = N_HEADS * HEAD_DIM
    kv_w = N_KV_HEADS * HEAD_DIM

    o_q = 2 * lru_w
    o_k = o_q + attn_w
    o_ga = o_k + 2 * kv_w
    w_in_r = jnp.concatenate([w_in[:, :, o_ga:], w_in[:, :, :o_k], w_in[:, :, o_k:o_ga]], axis=-1).astype(BF16)
    gates_w = 2 * d
    xr_block = gates_w // lru_w
    gr_block = xr_block + 1
    q_block = (gates_w + 2 * lru_w) // attn_w
    kv_block = (gates_w + 2 * lru_w + attn_w) // (2 * kv_w)
    up1, dn1 = w_ffn1_up.astype(BF16), w_ffn1_down.astype(BF16)
    up2, dn2 = w_ffn2_up.astype(BF16), w_ffn2_down.astype(BF16)
    wbr, wba, wo = w_br_rnn.astype(BF16), w_br_attn.astype(BF16), w_out.astype(BF16)
    wa_b, wx_b = lru_wa.astype(BF16), lru_wx.astype(BF16)

    vec3 = lambda a: a.reshape(depth, 1, a.shape[-1])
    g1, gm, g2 = vec3(g_ffn1), vec3(g_mix), vec3(g_ffn2)
    cb, ba, bx, lam = vec3(conv_b), vec3(lru_ba), vec3(lru_bx), vec3(lru_lambda)
    gf = g_final.reshape(1, d)

    mod_rows = 2 * SUBLANES
    c_pad = jnp.pad(c, ((0, mod_rows - batch), (0, 0)))
    mod = _modulation(c_pad, w_mod, b_mod)

    xs = x.reshape(batch * seq, d)
    for l in range(depth):
        mod_l = mod[l].reshape(mod_rows, 1, N_MOD * d)
        xs = _ffn(xs, g1, mod_l, 0, up1, dn1, l, seq, gf, False)
        z = _inproj(xs, gm, mod_l, w_in_r, l, seq)
        rnn = _lru(z, conv_w, cb, wa_b, ba, wx_b, bx, lam, l, batch, seq, xr_block, gr_block)
        attn = _attention(z, attn_sinks, l, batch, seq, q_block, kv_block)
        xs = _merge(xs, rnn, attn, z, mod_l, wbr, wba, wo, l, seq)
        xs = _ffn(xs, g2, mod_l, 6, up2, dn2, l, seq, gf, l == depth - 1)
    return xs.reshape(batch, seq, d)
```

```python
import functools

import jax
import jax.numpy as jnp
from jax import lax
from jax.experimental import pallas as pl
from jax.experimental.pallas import tpu as pltpu

F32 = jnp.float32
BF16 = jnp.bfloat16

N_HEADS = 16
N_KV_HEADS = 2
HEAD_DIM = 64
WINDOW = 128
LRU_C = 8.0
FFN_RES = 0.5
EPS = 1e-6
N_MOD = 9
SOFTMAX_SCALE = HEAD_DIM ** -0.5
MASK_VALUE = -1e30

VMEM_LIMIT_BYTES = 56 * 1024 * 1024
SUBLANES = 8
LANES = 128

MOD_TN = 1024
NORM_TM = 512
FFN_TM = 1024
FFN_TF = 256
FFN_XROWS = 128
FFN_DOWN_TN = 512
INPROJ_TM = 1024
INPROJ_TN = 1024
LRU_TC = 256
ATTN_TQ = 512
MERGE_TM = 256


def _params(*sem):
    return pltpu.CompilerParams(dimension_semantics=sem, vmem_limit_bytes=VMEM_LIMIT_BYTES)


def _sigmoid(x):
    return 1.0 / (1.0 + jnp.exp(-x))


def _gelu_tanh(x):
    c = 0.7978845608028654
    return 0.5 * x * (1.0 + jnp.tanh(c * (x + 0.044715 * (x * x * x))))


def _rmsnorm(x, g):
    ms = jnp.mean(x * x, axis=-1, keepdims=True)
    return x * lax.rsqrt(ms + EPS) * g


def _normmod(x, g, shift, scale):
    return _rmsnorm(x, g) * (1.0 + scale) + shift


def _mod_kernel(c_ref, w_ref, b_ref, o_ref):
    c = c_ref[...]
    ca = (c * _sigmoid(c)).astype(BF16)
    o_ref[...] = jnp.dot(ca, w_ref[...].astype(BF16), preferred_element_type=F32) + b_ref[...]


def _modulation(c_pad, w_mod, b_mod):
    depth, d, nd = w_mod.shape
    rows = c_pad.shape[0]
    return pl.pallas_call(
        _mod_kernel,
        grid=(depth, nd // MOD_TN),
        in_specs=[
            pl.BlockSpec((rows, d), lambda l, j: (0, 0)),
            pl.BlockSpec((None, d, MOD_TN), lambda l, j: (l, 0, j)),
            pl.BlockSpec((None, 1, MOD_TN), lambda l, j: (l, 0, j)),
        ],
        out_specs=pl.BlockSpec((None, rows, MOD_TN), lambda l, j: (l, 0, j)),
        out_shape=jax.ShapeDtypeStruct((depth, rows, nd), F32),
        compiler_params=_params("arbitrary", "arbitrary"),
        name="modulation",
    )(c_pad, w_mod, b_mod.reshape(depth, 1, nd))


def _mod_spec(d, batch_of, chunk):
    return pl.BlockSpec((None, 1, d), lambda *idx: (batch_of(*idx), 0, chunk))


def _normmod_kernel(x_ref, g_ref, sh_ref, sc_ref, o_ref):
    o_ref[...] = _normmod(x_ref[...], g_ref[...], sh_ref[...], sc_ref[...]).astype(o_ref.dtype)


def _normmod_call(x, g, mod_l, chunk0, layer, seq):
    tok, d = x.shape
    tpb = seq // NORM_TM
    return pl.pallas_call(
        _normmod_kernel,
        grid=(tok // NORM_TM,),
        in_specs=[
            pl.BlockSpec((NORM_TM, d), lambda i: (i, 0)),
            pl.BlockSpec((None, 1, d), lambda i: (layer, 0, 0)),
            _mod_spec(d, lambda i: i // tpb, chunk0),
            _mod_spec(d, lambda i: i // tpb, chunk0 + 1),
        ],
        out_specs=pl.BlockSpec((NORM_TM, d), lambda i: (i, 0)),
        out_shape=jax.ShapeDtypeStruct((tok, d), BF16),
        compiler_params=_params("arbitrary"),
        name="normmod",
    )(x, g, mod_l, mod_l)


def _ffn_kernel(h_ref, x_ref, ga_ref, wg_ref, wu_ref, wd_ref, gf_ref, o_ref, act_ref, *, nf, final_norm):
    s = pl.program_id(0)
    fd = jnp.maximum(s - 1, 0) % nf
    slot = s % 2
    tm, d = o_ref.shape
    nx = tm // FFN_XROWS
    tf = wg_ref.shape[1]

    @pl.when(s == 0)
    def _():
        act_ref[...] = jnp.zeros_like(act_ref)

    @pl.when(fd == 0)
    def _():
        o_ref[...] = jnp.zeros_like(o_ref)

    act_prev = act_ref[1 - slot]
    wd = wd_ref[...].astype(BF16)
    coef = FFN_RES * ga_ref[...]
    for n in range(d // FFN_DOWN_TN):
        cs = slice(n * FFN_DOWN_TN, (n + 1) * FFN_DOWN_TN)
        o_ref[:, cs] += coef[:, cs] * jnp.dot(act_prev, wd[:, cs], preferred_element_type=F32)

    r0 = pl.multiple_of(jnp.minimum(fd, nx - 1) * FFN_XROWS, FFN_XROWS)
    take = jnp.where(fd < nx, 1.0, 0.0)
    o_ref[pl.ds(r0, FFN_XROWS), :] += take * x_ref[...]

    w = jnp.concatenate([wg_ref[...].astype(BF16), wu_ref[...].astype(BF16)], axis=1)
    gu = jnp.dot(h_ref[...], w, preferred_element_type=F32)
    gate, up = gu[:, :tf], gu[:, tf:]
    act_ref[slot] = (gate * _sigmoid(gate) * up).astype(BF16)

    if final_norm:
        @pl.when(fd == nf - 1)
        def _():
            o_ref[...] = _rmsnorm(o_ref[...], gf_ref[...])


def _ffn(h, x, mod_l, gate_chunk, w_up, w_down, layer, seq, g_final, final_norm):
    tok, d = x.shape
    dff = w_down.shape[1]
    nf = dff // FFN_TF
    n_chunks = (tok // FFN_TM) * nf
    tpb = seq // FFN_TM
    nx = FFN_TM // FFN_XROWS
    up_step = lambda s: jnp.minimum(s, n_chunks - 1)
    down_step = lambda s: jnp.maximum(s - 1, 0)
    kern = functools.partial(_ffn_kernel, nf=nf, final_norm=final_norm)
    return pl.pallas_call(
        kern,
        grid=(n_chunks + 1,),
        in_specs=[
            pl.BlockSpec((FFN_TM, d), lambda s: (up_step(s) // nf, 0)),
            pl.BlockSpec((FFN_XROWS, d),
                         lambda s: ((down_step(s) // nf) * nx + jnp.minimum(down_step(s) % nf, nx - 1), 0)),
            _mod_spec(d, lambda s: (down_step(s) // nf) // tpb, gate_chunk),
            pl.BlockSpec((None, d, FFN_TF), lambda s: (layer, 0, up_step(s) % nf)),
            pl.BlockSpec((None, d, FFN_TF), lambda s: (layer, 0, nf + up_step(s) % nf)),
            pl.BlockSpec((None, FFN_TF, d), lambda s: (layer, down_step(s) % nf, 0)),
            pl.BlockSpec((1, d), lambda s: (0, 0)),
        ],
        out_specs=pl.BlockSpec((FFN_TM, d), lambda s: (down_step(s) // nf, 0)),
        out_shape=jax.ShapeDtypeStruct((tok, d), F32),
        scratch_shapes=[pltpu.VMEM((2, FFN_TM, FFN_TF), BF16)],
        compiler_params=_params("arbitrary"),
        name="ffn",
    )(h, x, mod_l, w_up, w_up, w_down, g_final)


INPROJ_SIGMOID_TILES = 4
INPROJ_GELU_TILE = 5
INPROJ_WSPLIT = 4


def _inproj_kernel(h_ref, *refs, n_full, tail):
    w_refs, (o_ref, wb_ref) = refs[:INPROJ_WSPLIT], refs[INPROJ_WSPLIT:]
    n = pl.program_id(0)
    sub = wb_ref.shape[1] // INPROJ_WSPLIT

    @pl.when(pl.program_id(1) == 0)
    def _():
        for k, w_ref in enumerate(w_refs):
            wb_ref[:, k * sub:(k + 1) * sub] = w_ref[...].astype(BF16)

    h = h_ref[...]

    @pl.when(n < INPROJ_SIGMOID_TILES)
    def _():
        o_ref[...] = _sigmoid(jnp.dot(h, wb_ref[...], preferred_element_type=F32))

    @pl.when(n == INPROJ_GELU_TILE)
    def _():
        o_ref[...] = _gelu_tanh(jnp.dot(h, wb_ref[...], preferred_element_type=F32))

    @pl.when((n >= INPROJ_SIGMOID_TILES) & (n != INPROJ_GELU_TILE) & (n < n_full))
    def _():
        o_ref[...] = jnp.dot(h, wb_ref[...], preferred_element_type=F32)

    @pl.when(n == n_full)
    def _():
        o_ref[:, :tail] = jnp.dot(h, wb_ref[:, :tail], preferred_element_type=F32)


def _inproj(h, w_in, layer, gate_col0):
    tok, d = h.shape
    ncol = w_in.shape[2]
    n_full = ncol // INPROJ_TN
    tail = ncol - n_full * INPROJ_TN
    sub = INPROJ_TN // INPROJ_WSPLIT
    gate_blk0 = gate_col0 // sub
    last_blk = ncol // sub - 1

    def w_spec(k):
        def index(n, i):
            base = jnp.where(n < INPROJ_SIGMOID_TILES, gate_blk0 + INPROJ_WSPLIT * n,
                             INPROJ_WSPLIT * (n - INPROJ_SIGMOID_TILES))
            return (layer, 0, jnp.minimum(base + k, last_blk))
        return pl.BlockSpec((None, d, sub), index)

    kern = functools.partial(_inproj_kernel, n_full=n_full, tail=tail)
    return pl.pallas_call(
        kern,
        grid=(n_full + 1, tok // INPROJ_TM),
        in_specs=[pl.BlockSpec((INPROJ_TM, d), lambda n, i: (i, 0))] + [w_spec(k) for k in range(INPROJ_WSPLIT)],
        out_specs=pl.BlockSpec((INPROJ_TM, INPROJ_TN), lambda n, i: (i, n)),
        out_shape=jax.ShapeDtypeStruct((tok, ncol), F32),
        scratch_shapes=[pltpu.VMEM((d, INPROJ_TN), BF16)],
        compiler_params=_params("arbitrary", "arbitrary"),
        name="inproj",
    )(h, *([w_in] * INPROJ_WSPLIT))


def _lru_kernel(xr_ref, gg_ref, cw_ref, cb_ref, wa_ref, ba_ref, wx_ref, bx_ref, lam_ref, o_ref,
                ext_ref, h_ref, a_ref, b_ref, hs_ref):
    tc, width = xr_ref.shape
    t = pl.program_id(1)

    @pl.when(t == 0)
    def _():
        ext_ref[0:SUBLANES, :] = jnp.zeros((SUBLANES, width), F32)
        h_ref[...] = jnp.zeros_like(h_ref)

    xr = xr_ref[...]
    ext_ref[SUBLANES:SUBLANES + tc, :] = xr
    kw = cw_ref.shape[0]
    u = cb_ref[...] + cw_ref[kw - 1:kw, :] * xr
    for k in range(kw - 1):
        back = kw - 1 - k
        u = u + cw_ref[k:k + 1, :] * ext_ref[SUBLANES - back:SUBLANES - back + tc, :]
    ext_ref[0:SUBLANES, :] = xr[tc - SUBLANES:tc, :]

    lam = lam_ref[...]
    nlam = -lam
    softplus = jnp.maximum(nlam, 0.0) + jnp.log1p(jnp.exp(-jnp.abs(nlam)))
    nb, bw, _ = wa_ref.shape
    for n in range(nb):
        sl = slice(n * bw, (n + 1) * bw)
        un = u[:, sl]
        unb = un.astype(BF16)
        r = _sigmoid(jnp.dot(unb, wa_ref[n], preferred_element_type=F32) + ba_ref[:, sl])
        ig = _sigmoid(jnp.dot(unb, wx_ref[n], preferred_element_type=F32) + bx_ref[:, sl])
        log_a = -LRU_C * r * softplus[:, sl]
        a = jnp.exp(log_a)
        a_ref[:, sl] = a
        b_ref[:, sl] = jnp.sqrt(-jnp.tanh(log_a) * (a * a + 1.0)) * (ig * un)

    row = lax.broadcasted_iota(jnp.int32, (SUBLANES, width), 0)

    def body(k, h):
        off = pl.multiple_of(k * SUBLANES, SUBLANES)
        a = a_ref[pl.ds(off, SUBLANES), :]
        b = b_ref[pl.ds(off, SUBLANES), :]
        for dist in (1, 2, 4):
            keep = row >= dist
            b = b + a * jnp.where(keep, pltpu.roll(b, dist, 0), 0.0)
            a = a * jnp.where(keep, pltpu.roll(a, dist, 0), 1.0)
        hs = a * h + b
        hs_ref[pl.ds(off, SUBLANES), :] = hs
        return hs[SUBLANES - 1:SUBLANES, :]

    h_ref[...] = lax.fori_loop(0, tc // SUBLANES, body, h_ref[...])
    o_ref[...] = (gg_ref[...] * hs_ref[...]).astype(o_ref.dtype)


def _lru(z, conv_w, conv_b, wa, ba, wx, bx, lam, layer, batch, seq, xr_block, gr_block):
    tok = z.shape[0]
    width = conv_w.shape[2]
    nt = seq // LRU_TC
    vec = lambda: pl.BlockSpec((None, 1, width), lambda b, t: (layer, 0, 0))
    blk = lambda: pl.BlockSpec((None,) + wa.shape[1:], lambda b, t: (layer, 0, 0, 0))
    return pl.pallas_call(
        _lru_kernel,
        grid=(batch, nt),
        in_specs=[
            pl.BlockSpec((LRU_TC, width), lambda b, t: (b * nt + t, xr_block)),
            pl.BlockSpec((LRU_TC, width), lambda b, t: (b * nt + t, gr_block)),
            pl.BlockSpec((None,) + conv_w.shape[1:], lambda b, t: (layer, 0, 0)),
            vec(), blk(), vec(), blk(), vec(), vec(),
        ],
        out_specs=pl.BlockSpec((LRU_TC, width), lambda b, t: (b * nt + t, 0)),
        out_shape=jax.ShapeDtypeStruct((tok, width), BF16),
        scratch_shapes=[
            pltpu.VMEM((LRU_TC + SUBLANES, width), F32),
            pltpu.VMEM((1, width), F32),
            pltpu.VMEM((LRU_TC, width), F32),
            pltpu.VMEM((LRU_TC, width), F32),
            pltpu.VMEM((LRU_TC, width), F32),
        ],
        compiler_params=_params("arbitrary", "arbitrary"),
        name="rglru",
    )(z, z, conv_w, conv_b, wa, ba, wx, bx, lam)


def _attn_kernel(sink_ref, q_ref, kvo_ref, kvp_ref, o_ref, *, layer):
    tq = q_ref.shape[0]
    first_valid_key = jnp.where(pl.program_id(1) == 0, WINDOW, 0)
    pairs = N_HEADS // N_KV_HEADS // 2
    rows = pairs * WINDOW
    lane = lax.broadcasted_iota(jnp.int32, (2 * WINDOW, LANES), 1)
    lo = lane < HEAD_DIM
    qpos = lax.broadcasted_iota(jnp.int32, (rows, 2 * WINDOW), 0) % WINDOW
    kpos = lax.broadcasted_iota(jnp.int32, (rows, 2 * WINDOW), 1)
    band = (kpos > qpos) & (kpos <= qpos + WINDOW)
    pair_of_row = lax.broadcasted_iota(jnp.int32, (rows, 1), 0) // WINDOW
    out_lo = lax.broadcasted_iota(jnp.int32, (rows, LANES), 1) < HEAD_DIM

    for jb in range(tq // WINDOW):
        own = kvo_ref[jb * WINDOW:(jb + 1) * WINDOW, :]
        prev = kvp_ref[...] if jb == 0 else kvo_ref[(jb - 1) * WINDOW:jb * WINDOW, :]
        kv = jnp.concatenate([prev, own], axis=0)
        kk, vv = kv[:, :LANES], kv[:, LANES:]
        kk_rot = pltpu.roll(kk, HEAD_DIM, 1)
        vv_rot = pltpu.roll(vv, HEAD_DIM, 1)
        if jb == 0:
            mask = band & (kpos >= first_valid_key)
        else:
            mask = band
        for kvh in range(N_KV_HEADS):
            if kvh == 0:
                k_top, k_bot, v_top, v_bot = kk, kk_rot, vv, vv_rot
            else:
                k_top, k_bot, v_top, v_bot = kk_rot, kk, vv_rot, vv
            kbd = jnp.concatenate([jnp.where(lo, k_top, 0.0), jnp.where(lo, 0.0, k_bot)], axis=0).astype(BF16)
            vbd = jnp.concatenate([jnp.where(lo, v_top, 0.0), jnp.where(lo, 0.0, v_bot)], axis=0).astype(BF16)
            qs = jnp.concatenate(
                [q_ref[jb * WINDOW:(jb + 1) * WINDOW, (pairs * kvh + p) * LANES:(pairs * kvh + p + 1) * LANES]
                 for p in range(pairs)], axis=0).astype(BF16)
            s = lax.dot_general(qs, kbd, (((1,), (1,)), ((), ())), preferred_element_type=F32)
            s = s * SOFTMAX_SCALE
            probs, inv = [], []
            for half in range(2):
                sh = jnp.where(mask, s[:, half * 2 * WINDOW:(half + 1) * 2 * WINDOW], MASK_VALUE)
                sink = jnp.zeros((rows, 1), F32)
                for p in range(pairs):
                    head = (N_HEADS // N_KV_HEADS) * kvh + 2 * p + half
                    sink = jnp.where(pair_of_row == p, sink_ref[layer, head], sink)
                m = jnp.maximum(jnp.max(sh, axis=-1, keepdims=True), sink)
                e = jnp.exp(sh - m)
                den = jnp.sum(e, axis=-1, keepdims=True) + jnp.exp(sink - m)
                probs.append(e.astype(BF16))
                inv.append(1.0 / den)
            pv = jnp.dot(jnp.concatenate(probs, axis=1), vbd, preferred_element_type=F32)
            pv = pv * jnp.where(out_lo, inv[0], inv[1])
            for p in range(pairs):
                col = (pairs * kvh + p) * LANES
                o_ref[jb * WINDOW:(jb + 1) * WINDOW, col:col + LANES] = (
                    pv[p * WINDOW:(p + 1) * WINDOW, :].astype(o_ref.dtype))


def _attention(z, sinks, layer, batch, seq, q_block, kv_block):
    tok = z.shape[0]
    width = N_HEADS * HEAD_DIM
    kvw = 2 * N_KV_HEADS * HEAD_DIM
    nt = seq // ATTN_TQ
    per = ATTN_TQ // WINDOW
    kern = functools.partial(_attn_kernel, layer=layer)
    return pl.pallas_call(
        kern,
        grid=(batch, nt),
        in_specs=[
            pl.BlockSpec(memory_space=pltpu.SMEM),
            pl.BlockSpec((ATTN_TQ, width), lambda b, i: (b * nt + i, q_block)),
            pl.BlockSpec((ATTN_TQ, kvw), lambda b, i: (b * nt + i, kv_block)),
            pl.BlockSpec((WINDOW, kvw), lambda b, i: (jnp.maximum((b * nt + i) * per - 1, 0), kv_block)),
        ],
        out_specs=pl.BlockSpec((ATTN_TQ, width), lambda b, i: (b * nt + i, 0)),
        out_shape=jax.ShapeDtypeStruct((tok, width), BF16),
        compiler_params=_params("arbitrary", "arbitrary"),
        name="swattn",
    )(sinks, z, z, z)


def _merge_kernel(x_ref, rnn_ref, attn_ref, sga_ref, sgb_ref, g2_ref, wr_ref, wa_ref, wo_ref, o_ref):
    br = jnp.dot(rnn_ref[...], wr_ref[...], preferred_element_type=F32)
    ba = jnp.dot(attn_ref[...], wa_ref[...], preferred_element_type=F32)
    merged = (sga_ref[...] * br + sgb_ref[...] * ba).astype(BF16)
    o_ref[...] = x_ref[...] + g2_ref[...] * jnp.dot(merged, wo_ref[...], preferred_element_type=F32)


def _merge(x, rnn, attn, z, mod_l, w_br_rnn, w_br_attn, w_out, layer, seq):
    tok, d = x.shape
    width = rnn.shape[1]
    tpb = seq // MERGE_TM
    resident = pl.Buffered(1)
    return pl.pallas_call(
        _merge_kernel,
        grid=(tok // MERGE_TM,),
        in_specs=[
            pl.BlockSpec((MERGE_TM, d), lambda i: (i, 0)),
            pl.BlockSpec((MERGE_TM, width), lambda i: (i, 0)),
            pl.BlockSpec((MERGE_TM, width), lambda i: (i, 0)),
            pl.BlockSpec((MERGE_TM, d), lambda i: (i, 0)),
            pl.BlockSpec((MERGE_TM, d), lambda i: (i, 1)),
            pl.BlockSpec((None, 1, d), lambda i: (i // tpb, 0, 5)),
            pl.BlockSpec((None, width, d), lambda i: (layer, 0, 0), pipeline_mode=resident),
            pl.BlockSpec((None, width, d), lambda i: (layer, 0, 0), pipeline_mode=resident),
            pl.BlockSpec((None, d, d), lambda i: (layer, 0, 0), pipeline_mode=resident),
        ],
        out_specs=pl.BlockSpec((MERGE_TM, d), lambda i: (i, 0)),
        out_shape=jax.ShapeDtypeStruct((tok, d), F32),
        compiler_params=_params("arbitrary"),
        name="merge",
    )(x, rnn, attn, z, z, mod_l, w_br_rnn, w_br_attn, w_out)


def kernel(x, c, g_ffn1, w_ffn1_up, w_ffn1_down, g_mix, w_in, conv_w, conv_b, lru_wa, lru_ba, lru_wx,
           lru_bx, lru_lambda, attn_sinks, w_br_rnn, w_br_attn, w_out, g_ffn2, w_ffn2_up, w_ffn2_down,
           w_mod, b_mod, g_final):
    batch, seq, d = x.shape
    depth = w_mod.shape[0]
    lru_w = conv_w.shape[2]
    attn_w = N_HEADS * HEAD_DIM
    kv_w = N_KV_HEADS * HEAD_DIM

    gate_col0 = 2 * lru_w + attn_w + 2 * kv_w
    gates_w = 2 * d
    xr_block = gates_w // lru_w
    gr_block = xr_block + 1
    q_block = (gates_w + 2 * lru_w) // attn_w
    kv_block = (gates_w + 2 * lru_w + attn_w) // (2 * kv_w)
    wbr, wba, wo = w_br_rnn.astype(BF16), w_br_attn.astype(BF16), w_out.astype(BF16)
    wa_b, wx_b = lru_wa.astype(BF16), lru_wx.astype(BF16)

    vec3 = lambda a: a.reshape(depth, 1, a.shape[-1])
    g1, gm, g2 = vec3(g_ffn1), vec3(g_mix), vec3(g_ffn2)
    cb, ba, bx, lam = vec3(conv_b), vec3(lru_ba), vec3(lru_bx), vec3(lru_lambda)
    gf = g_final.reshape(1, d)

    mod_rows = 2 * SUBLANES
    c_pad = jnp.pad(c, ((0, mod_rows - batch), (0, 0)))
    mod = _modulation(c_pad, w_mod, b_mod)

    xs = x.reshape(batch * seq, d)
    for l in range(depth):
        mod_l = mod[l].reshape(mod_rows, 1, N_MOD * d)
        h = _normmod_call(xs, g1, mod_l, 0, l, seq)
        xs = _ffn(h, xs, mod_l, 2, w_ffn1_up, w_ffn1_down, l, seq, gf, False)
        h = _normmod_call(xs, gm, mod_l, 3, l, seq)
        z = _inproj(h, w_in, l, gate_col0)
        rnn = _lru(z, conv_w, cb, wa_b, ba, wx_b, bx, lam, l, batch, seq, xr_block, gr_block)
        attn = _attention(z, attn_sinks, l, batch, seq, q_block, kv_block)
        xs = _merge(xs, rnn, attn, z, mod_l, wbr, wba, wo, l, seq)
        h = _normmod_call(xs, g2, mod_l, 6, l, seq)
        xs = _ffn(h, xs, mod_l, 8, w_ffn2_up, w_ffn2_down, l, seq, gf, l == depth - 1)
    return xs.reshape(batch, seq, d)
```

```python
import functools

import jax
import jax.numpy as jnp
from jax import lax
from jax.experimental import pallas as pl
from jax.experimental.pallas import tpu as pltpu

F32 = jnp.float32
BF16 = jnp.bfloat16

N_HEADS = 16
N_KV_HEADS = 2
HEAD_DIM = 64
WINDOW = 128
LRU_C = 8.0
FFN_RES = 0.5
EPS = 1e-6
N_MOD = 9
SOFTMAX_SCALE = HEAD_DIM ** -0.5
assert SOFTMAX_SCALE == 2.0 ** -3
MASK_VALUE = -1e30

VMEM_LIMIT_BYTES = 56 * 1024 * 1024
SUBLANES = 8
LANES = 128

MOD_TN = 1024
NORM_TM = 512
NORM_ROWS = 32
FFN_TM = 1024
FFN_TF = 512
FFN_UP_SPLIT = 2
FFN_CAST_STEPS = 64
FFN_CAST_DOWN_ROWS = 128
FFN_XROWS = 128
FFN_DOWN_TN = 512
INPROJ_TM = 1024
INPROJ_TN = 1024
LRU_TC = 256
ATTN_TQ = 512
MERGE_TM = 256


def _params(*sem):
    return pltpu.CompilerParams(dimension_semantics=sem, vmem_limit_bytes=VMEM_LIMIT_BYTES)


def _sigmoid(x):
    return 0.5 * jnp.tanh(0.5 * x) + 0.5


def _gelu_tanh(x):
    c = 0.7978845608028654
    return 0.5 * x * (1.0 + jnp.tanh(c * (x + 0.044715 * (x * x * x))))


def _rmsnorm(x, g):
    ms = jnp.mean(x * x, axis=-1, keepdims=True)
    return x * lax.rsqrt(ms + EPS) * g


def _normmod(x, g, shift, scale):
    return _rmsnorm(x, g) * (1.0 + scale) + shift


def _mod_kernel(c_ref, w_ref, b_ref, o_ref):
    c = c_ref[...]
    ca = (c * _sigmoid(c)).astype(BF16)
    o_ref[...] = jnp.dot(ca, w_ref[...].astype(BF16), preferred_element_type=F32) + b_ref[...]


def _modulation(c_pad, w_mod, b_mod):
    depth, d, nd = w_mod.shape
    rows = c_pad.shape[0]
    return pl.pallas_call(
        _mod_kernel,
        grid=(depth, nd // MOD_TN),
        in_specs=[
            pl.BlockSpec((rows, d), lambda l, j: (0, 0)),
            pl.BlockSpec((None, d, MOD_TN), lambda l, j: (l, 0, j)),
            pl.BlockSpec((None, 1, MOD_TN), lambda l, j: (l, 0, j)),
        ],
        out_specs=pl.BlockSpec((None, rows, MOD_TN), lambda l, j: (l, 0, j)),
        out_shape=jax.ShapeDtypeStruct((depth, rows, nd), F32),
        compiler_params=_params("arbitrary", "arbitrary"),
        name="modulation",
    )(c_pad, w_mod, b_mod.reshape(depth, 1, nd))


def _mod_spec(d, batch_of, chunk):
    return pl.BlockSpec((None, 1, d), lambda *idx: (batch_of(*idx), 0, chunk))


def _normmod_kernel(x_ref, g_ref, sh_ref, sc_ref, o_ref):
    g, sh, sc = g_ref[...], sh_ref[...], sc_ref[...]

    def body(k, carry):
        rows = pl.ds(pl.multiple_of(k * NORM_ROWS, NORM_ROWS), NORM_ROWS)
        o_ref[rows, :] = _normmod(x_ref[rows, :], g, sh, sc).astype(o_ref.dtype)
        return carry

    lax.fori_loop(0, x_ref.shape[0] // NORM_ROWS, body, 0, unroll=2)


def _normmod_call(x, g, mod_l, chunk0, layer, seq):
    tok, d = x.shape
    tpb = seq // NORM_TM
    return pl.pallas_call(
        _normmod_kernel,
        grid=(tok // NORM_TM,),
        in_specs=[
            pl.BlockSpec((NORM_TM, d), lambda i: (i, 0)),
            pl.BlockSpec((None, 1, d), lambda i: (layer, 0, 0)),
            _mod_spec(d, lambda i: i // tpb, chunk0),
            _mod_spec(d, lambda i: i // tpb, chunk0 + 1),
        ],
        out_specs=pl.BlockSpec((NORM_TM, d), lambda i: (i, 0)),
        out_shape=jax.ShapeDtypeStruct((tok, d), BF16),
        compiler_params=_params("arbitrary"),
        name="normmod",
    )(x, g, mod_l, mod_l)


def _ffn_kernel(*refs, nf, n_tiles, final_norm, cast_next):
    (xa_ref, g_ref, sh_ref, sc_ref, x_ref, ga_ref, wg_ref, wu_ref, wd_ref, gf_ref), refs = refs[:10], refs[10:]
    if cast_next:
        (cui_ref, cdi_ref, o_ref, cuo_ref, cdo_ref), refs = refs[:5], refs[5:]
    else:
        o_ref, refs = refs[0], refs[1:]
    h_ref, act_ref = refs
    s = pl.program_id(0)
    tm, d = o_ref.shape
    nx = tm // FFN_XROWS
    t = s - nx
    tu = jnp.clip(t, 0, n_tiles * nf - 1)
    iu, fu = tu // nf, tu % nf
    fd = jnp.maximum(t - 1, 0) % nf
    slot = s % 2

    if cast_next:
        @pl.when(s < FFN_CAST_STEPS)
        def _():
            cuo_ref[...] = cui_ref[...].astype(BF16)
            cdo_ref[...] = cdi_ref[...].astype(BF16)

    def norm_ahead(hslot, rchunk):
        r0 = pl.multiple_of(rchunk * FFN_XROWS, FFN_XROWS)
        hn = _normmod(xa_ref[...], g_ref[...], sh_ref[...], sc_ref[...])
        h_ref[hslot, pl.ds(r0, FFN_XROWS), :] = hn.astype(BF16)

    @pl.when(t < 0)
    def _():
        norm_ahead(0, s)

    @pl.when(t >= 0)
    def _():
        @pl.when(t == 0)
        def _():
            act_ref[...] = jnp.zeros_like(act_ref)

        @pl.when(fd == 0)
        def _():
            o_ref[...] = jnp.zeros_like(o_ref)

        @pl.when((fu < nx) & (iu + 1 < n_tiles))
        def _():
            norm_ahead((iu + 1) % 2, fu)

        act_prev = act_ref[1 - slot]
        coef = FFN_RES * ga_ref[...]
        for n in range(d // FFN_DOWN_TN):
            cs = slice(n * FFN_DOWN_TN, (n + 1) * FFN_DOWN_TN)
            o_ref[:, cs] += coef[:, cs] * jnp.dot(act_prev, wd_ref[:, cs], preferred_element_type=F32)

        r0 = pl.multiple_of(jnp.minimum(fd, nx - 1) * FFN_XROWS, FFN_XROWS)
        take = jnp.where(fd < nx, 1.0, 0.0)
        o_ref[pl.ds(r0, FFN_XROWS), :] += take * x_ref[...]

        for m in range(FFN_UP_SPLIT):
            rs = slice(m * tm // FFN_UP_SPLIT, (m + 1) * tm // FFN_UP_SPLIT)
            hm = h_ref[iu % 2, rs, :]
            gate = jnp.dot(hm, wg_ref[...], preferred_element_type=F32)
            up = jnp.dot(hm, wu_ref[...], preferred_element_type=F32)
            act_ref[slot, rs, :] = (gate * _sigmoid(gate) * up).astype(BF16)

        if final_norm:
            @pl.when(fd == nf - 1)
            def _():
                o_ref[...] = _rmsnorm(o_ref[...], gf_ref[...])


def _ffn(x, g, g_layer, mod_l, chunk0, w_up, w_down, seq, g_final, final_norm, cast_src):
    tok, d = x.shape
    dff = w_down.shape[0]
    nf = dff // FFN_TF
    n_tiles = tok // FFN_TM
    n_chunks = n_tiles * nf
    n_steps = FFN_TM // FFN_XROWS + n_chunks + 1
    tpb = seq // FFN_TM
    nx = FFN_TM // FFN_XROWS
    up_chunk = lambda s: jnp.clip(s - nx, 0, n_chunks - 1)
    down_tile = lambda s: jnp.maximum(s - nx - 1, 0) // nf
    down_chunk = lambda s: jnp.maximum(s - nx - 1, 0) % nf
    ahead_tile = lambda s: jnp.where(s < nx, 0, jnp.minimum(up_chunk(s) // nf + 1, n_tiles - 1))
    ahead_rows = lambda s: jnp.where(s < nx, s, jnp.minimum(up_chunk(s) % nf, nx - 1))
    in_specs = [
        pl.BlockSpec((FFN_XROWS, d), lambda s: (ahead_tile(s) * nx + ahead_rows(s), 0)),
        pl.BlockSpec((None, 1, d), lambda s: (g_layer, 0, 0)),
        _mod_spec(d, lambda s: ahead_tile(s) // tpb, chunk0),
        _mod_spec(d, lambda s: ahead_tile(s) // tpb, chunk0 + 1),
        pl.BlockSpec((FFN_XROWS, d), lambda s: (down_tile(s) * nx + jnp.minimum(down_chunk(s), nx - 1), 0)),
        _mod_spec(d, lambda s: down_tile(s) // tpb, chunk0 + 2),
        pl.BlockSpec((d, FFN_TF), lambda s: (0, up_chunk(s) % nf)),
        pl.BlockSpec((d, FFN_TF), lambda s: (0, nf + up_chunk(s) % nf)),
        pl.BlockSpec((FFN_TF, d), lambda s: (down_chunk(s), 0)),
        pl.BlockSpec((1, d), lambda s: (0, 0)),
    ]
    args = [x, g, mod_l, mod_l, x, mod_l, w_up, w_up, w_down, g_final]
    out_specs = [pl.BlockSpec((FFN_TM, d), lambda s: (down_tile(s), 0))]
    out_shape = [jax.ShapeDtypeStruct((tok, d), F32)]
    if cast_src is not None:
        src_up, src_down, src_layer = cast_src
        up_rows, down_rows = d // FFN_CAST_STEPS, FFN_CAST_DOWN_ROWS
        assert FFN_CAST_STEPS <= n_steps and dff // down_rows <= FFN_CAST_STEPS
        up_blk = lambda s: jnp.minimum(s, FFN_CAST_STEPS - 1)
        down_blk = lambda s: jnp.minimum(s, dff // down_rows - 1)
        in_specs += [pl.BlockSpec((None, up_rows, 2 * dff), lambda s: (src_layer, up_blk(s), 0)),
                     pl.BlockSpec((None, down_rows, d), lambda s: (src_layer, down_blk(s), 0))]
        args += [src_up, src_down]
        out_specs += [pl.BlockSpec((up_rows, 2 * dff), lambda s: (up_blk(s), 0)),
                      pl.BlockSpec((down_rows, d), lambda s: (down_blk(s), 0))]
        out_shape += [jax.ShapeDtypeStruct((d, 2 * dff), BF16), jax.ShapeDtypeStruct((dff, d), BF16)]
    kern = functools.partial(_ffn_kernel, nf=nf, n_tiles=n_tiles, final_norm=final_norm,
                             cast_next=cast_src is not None)
    return pl.pallas_call(
        kern,
        grid=(n_steps,),
        in_specs=in_specs,
        out_specs=out_specs,
        out_shape=out_shape,
        scratch_shapes=[pltpu.VMEM((2, FFN_TM, d), BF16), pltpu.VMEM((2, FFN_TM, FFN_TF), BF16)],
        compiler_params=_params("arbitrary"),
        name="ffn",
    )(*args)


INPROJ_SIGMOID_TILES = 4
INPROJ_GELU_TILE = 5
INPROJ_WSPLIT = 4


def _inproj_kernel(h_ref, *refs, n_full, tail):
    w_refs, (o_ref, wb_ref) = refs[:INPROJ_WSPLIT], refs[INPROJ_WSPLIT:]
    n = pl.program_id(0)
    sub = wb_ref.shape[1] // INPROJ_WSPLIT

    @pl.when(pl.program_id(1) == 0)
    def _():
        for k, w_ref in enumerate(w_refs):
            wb_ref[:, k * sub:(k + 1) * sub] = w_ref[...].astype(BF16)

    h = h_ref[...]

    @pl.when(n < INPROJ_SIGMOID_TILES)
    def _():
        o_ref[...] = _sigmoid(jnp.dot(h, wb_ref[...], preferred_element_type=F32))

    @pl.when(n == INPROJ_GELU_TILE)
    def _():
        o_ref[...] = _gelu_tanh(jnp.dot(h, wb_ref[...], preferred_element_type=F32))

    @pl.when((n >= INPROJ_SIGMOID_TILES) & (n != INPROJ_GELU_TILE) & (n < n_full))
    def _():
        o_ref[...] = jnp.dot(h, wb_ref[...], preferred_element_type=F32)

    @pl.when(n == n_full)
    def _():
        o_ref[:, :tail] = jnp.dot(h, wb_ref[:, :tail], preferred_element_type=F32)


def _inproj(h, w_in, layer, gate_col0):
    tok, d = h.shape
    ncol = w_in.shape[2]
    n_full = ncol // INPROJ_TN
    tail = ncol - n_full * INPROJ_TN
    sub = INPROJ_TN // INPROJ_WSPLIT
    gate_blk0 = gate_col0 // sub
    last_blk = ncol // sub - 1

    def w_spec(k):
        def index(n, i):
            base = jnp.where(n < INPROJ_SIGMOID_TILES, gate_blk0 + INPROJ_WSPLIT * n,
                             INPROJ_WSPLIT * (n - INPROJ_SIGMOID_TILES))
            return (layer, 0, jnp.minimum(base + k, last_blk))
        return pl.BlockSpec((None, d, sub), index)

    kern = functools.partial(_inproj_kernel, n_full=n_full, tail=tail)
    return pl.pallas_call(
        kern,
        grid=(n_full + 1, tok // INPROJ_TM),
        in_specs=[pl.BlockSpec((INPROJ_TM, d), lambda n, i: (i, 0))] + [w_spec(k) for k in range(INPROJ_WSPLIT)],
        out_specs=pl.BlockSpec((INPROJ_TM, INPROJ_TN), lambda n, i: (i, n)),
        out_shape=jax.ShapeDtypeStruct((tok, ncol), F32),
        scratch_shapes=[pltpu.VMEM((d, INPROJ_TN), BF16)],
        compiler_params=_params("arbitrary", "arbitrary"),
        name="inproj",
    )(h, *([w_in] * INPROJ_WSPLIT))


def _lru_kernel(xr_ref, gg_ref, cw_ref, cb_ref, wa_ref, ba_ref, wx_ref, bx_ref, lam_ref, o_ref,
                ext_ref, h_ref, a_ref, b_ref, hs_ref):
    tc, width = xr_ref.shape
    t = pl.program_id(1)

    @pl.when(t == 0)
    def _():
        ext_ref[0:SUBLANES, :] = jnp.zeros((SUBLANES, width), F32)
        h_ref[...] = jnp.zeros_like(h_ref)

    xr = xr_ref[...]
    ext_ref[SUBLANES:SUBLANES + tc, :] = xr
    kw = cw_ref.shape[0]
    u = cb_ref[...] + cw_ref[kw - 1:kw, :] * xr
    for k in range(kw - 1):
        back = kw - 1 - k
        u = u + cw_ref[k:k + 1, :] * ext_ref[SUBLANES - back:SUBLANES - back + tc, :]
    ext_ref[0:SUBLANES, :] = xr[tc - SUBLANES:tc, :]

    lam = lam_ref[...]
    nlam = -lam
    softplus = jnp.maximum(nlam, 0.0) + jnp.log1p(jnp.exp(-jnp.abs(nlam)))
    decay = -LRU_C * softplus
    nb, bw, _ = wa_ref.shape
    for n in range(nb):
        sl = slice(n * bw, (n + 1) * bw)
        un = u[:, sl]
        unb = un.astype(BF16)
        r = _sigmoid(jnp.dot(unb, wa_ref[n], preferred_element_type=F32) + ba_ref[:, sl])
        ig = _sigmoid(jnp.dot(unb, wx_ref[n], preferred_element_type=F32) + bx_ref[:, sl])
        log_a = r * decay[:, sl]
        a = jnp.exp(log_a)
        a_ref[:, sl] = a
        b_ref[:, sl] = jnp.sqrt(-jnp.tanh(log_a) * (a * a + 1.0)) * (ig * un)

    row = lax.broadcasted_iota(jnp.int32, (SUBLANES, width), 0)

    def body(k, h):
        off = pl.multiple_of(k * SUBLANES, SUBLANES)
        a = a_ref[pl.ds(off, SUBLANES), :]
        b = b_ref[pl.ds(off, SUBLANES), :]
        for dist in (1, 2, 4):
            keep = row >= dist
            b = b + a * jnp.where(keep, pltpu.roll(b, dist, 0), 0.0)
            a = a * jnp.where(keep, pltpu.roll(a, dist, 0), 1.0)
        hs = a * h + b
        hs_ref[pl.ds(off, SUBLANES), :] = hs
        return hs[SUBLANES - 1:SUBLANES, :]

    h_ref[...] = lax.fori_loop(0, tc // SUBLANES, body, h_ref[...])
    o_ref[...] = (gg_ref[...] * hs_ref[...]).astype(o_ref.dtype)


def _lru(z, conv_w, conv_b, wa, ba, wx, bx, lam, layer, batch, seq, xr_block, gr_block):
    tok = z.shape[0]
    width = conv_w.shape[2]
    nt = seq // LRU_TC
    vec = lambda: pl.BlockSpec((None, 1, width), lambda b, t: (layer, 0, 0))
    blk = lambda: pl.BlockSpec((None,) + wa.shape[1:], lambda b, t: (layer, 0, 0, 0))
    return pl.pallas_call(
        _lru_kernel,
        grid=(batch, nt),
        in_specs=[
            pl.BlockSpec((LRU_TC, width), lambda b, t: (b * nt + t, xr_block)),
            pl.BlockSpec((LRU_TC, width), lambda b, t: (b * nt + t, gr_block)),
            pl.BlockSpec((None,) + conv_w.shape[1:], lambda b, t: (layer, 0, 0)),
            vec(), blk(), vec(), blk(), vec(), vec(),
        ],
        out_specs=pl.BlockSpec((LRU_TC, width), lambda b, t: (b * nt + t, 0)),
        out_shape=jax.ShapeDtypeStruct((tok, width), BF16),
        scratch_shapes=[
            pltpu.VMEM((LRU_TC + SUBLANES, width), F32),
            pltpu.VMEM((1, width), F32),
            pltpu.VMEM((LRU_TC, width), F32),
            pltpu.VMEM((LRU_TC, width), F32),
            pltpu.VMEM((LRU_TC, width), F32),
        ],
        compiler_params=_params("arbitrary", "arbitrary"),
        name="rglru",
    )(z, z, conv_w, conv_b, wa, ba, wx, bx, lam)


def _attn_kernel(sink_ref, q_ref, kvo_ref, kvp_ref, o_ref, *, layer):
    tq = q_ref.shape[0]
    first_valid_key = jnp.where(pl.program_id(1) == 0, WINDOW, 0)
    pairs = N_HEADS // N_KV_HEADS // 2
    rows = pairs * WINDOW
    lane = lax.broadcasted_iota(jnp.int32, (2 * WINDOW, LANES), 1)
    lo = lane < HEAD_DIM
    qpos = lax.broadcasted_iota(jnp.int32, (rows, 2 * WINDOW), 0) % WINDOW
    kpos = lax.broadcasted_iota(jnp.int32, (rows, 2 * WINDOW), 1)
    band = (kpos > qpos) & (kpos <= qpos + WINDOW)
    pair_of_row = lax.broadcasted_iota(jnp.int32, (rows, 1), 0) // WINDOW
    out_lo = lax.broadcasted_iota(jnp.int32, (rows, LANES), 1) < HEAD_DIM

    for jb in range(tq // WINDOW):
        own = kvo_ref[jb * WINDOW:(jb + 1) * WINDOW, :]
        prev = kvp_ref[...] if jb == 0 else kvo_ref[(jb - 1) * WINDOW:jb * WINDOW, :]
        kv = jnp.concatenate([prev, own], axis=0)
        kk, vv = kv[:, :LANES], kv[:, LANES:]
        kk_rot = pltpu.roll(kk, HEAD_DIM, 1)
        vv_rot = pltpu.roll(vv, HEAD_DIM, 1)
        if jb == 0:
            mask = band & (kpos >= first_valid_key)
        else:
            mask = band
        for kvh in range(N_KV_HEADS):
            if kvh == 0:
                k_top, k_bot, v_top, v_bot = kk, kk_rot, vv, vv_rot
            else:
                k_top, k_bot, v_top, v_bot = kk_rot, kk, vv_rot, vv
            kbd = jnp.concatenate([jnp.where(lo, k_top, 0.0), jnp.where(lo, 0.0, k_bot)], axis=0).astype(BF16)
            vbd = jnp.concatenate([jnp.where(lo, v_top, 0.0), jnp.where(lo, 0.0, v_bot)], axis=0).astype(BF16)
            qs = jnp.concatenate(
                [q_ref[jb * WINDOW:(jb + 1) * WINDOW, (pairs * kvh + p) * LANES:(pairs * kvh + p + 1) * LANES]
                 for p in range(pairs)], axis=0)
            qs = (qs * SOFTMAX_SCALE).astype(BF16)
            s = lax.dot_general(qs, kbd, (((1,), (1,)), ((), ())), preferred_element_type=F32)
            probs, inv = [], []
            for half in range(2):
                sh = jnp.where(mask, s[:, half * 2 * WINDOW:(half + 1) * 2 * WINDOW], MASK_VALUE)
                sink = jnp.zeros((rows, 1), F32)
                for p in range(pairs):
                    head = (N_HEADS // N_KV_HEADS) * kvh + 2 * p + half
                    sink = jnp.where(pair_of_row == p, sink_ref[layer, head], sink)
                m = jnp.maximum(jnp.max(sh, axis=-1, keepdims=True), sink)
                e = jnp.exp(sh - m)
                den = jnp.sum(e, axis=-1, keepdims=True) + jnp.exp(sink - m)
                probs.append(e.astype(BF16))
                inv.append(1.0 / den)
            pv = jnp.dot(jnp.concatenate(probs, axis=1), vbd, preferred_element_type=F32)
            pv = pv * jnp.where(out_lo, inv[0], inv[1])
            for p in range(pairs):
                col = (pairs * kvh + p) * LANES
                o_ref[jb * WINDOW:(jb + 1) * WINDOW, col:col + LANES] = (
                    pv[p * WINDOW:(p + 1) * WINDOW, :].astype(o_ref.dtype))


def _attention(z, sinks, layer, batch, seq, q_block, kv_block):
    tok = z.shape[0]
    width = N_HEADS * HEAD_DIM
    kvw = 2 * N_KV_HEADS * HEAD_DIM
    nt = seq // ATTN_TQ
    per = ATTN_TQ // WINDOW
    kern = functools.partial(_attn_kernel, layer=layer)
    return pl.pallas_call(
        kern,
        grid=(batch, nt),
        in_specs=[
            pl.BlockSpec(memory_space=pltpu.SMEM),
            pl.BlockSpec((ATTN_TQ, width), lambda b, i: (b * nt + i, q_block)),
            pl.BlockSpec((ATTN_TQ, kvw), lambda b, i: (b * nt + i, kv_block)),
            pl.BlockSpec((WINDOW, kvw), lambda b, i: (jnp.maximum((b * nt + i) * per - 1, 0), kv_block)),
        ],
        out_specs=pl.BlockSpec((ATTN_TQ, width), lambda b, i: (b * nt + i, 0)),
        out_shape=jax.ShapeDtypeStruct((tok, width), BF16),
        compiler_params=_params("arbitrary", "arbitrary"),
        name="swattn",
    )(sinks, z, z, z)


def _merge_kernel(x_ref, rnn_ref, attn_ref, sga_ref, sgb_ref, g2_ref, wr_ref, wa_ref, wo_ref, o_ref):
    br = jnp.dot(rnn_ref[...], wr_ref[...], preferred_element_type=F32)
    ba = jnp.dot(attn_ref[...], wa_ref[...], preferred_element_type=F32)
    merged = (sga_ref[...] * br + sgb_ref[...] * ba).astype(BF16)
    o_ref[...] = x_ref[...] + g2_ref[...] * jnp.dot(merged, wo_ref[...], preferred_element_type=F32)


def _merge(x, rnn, attn, z, mod_l, w_br_rnn, w_br_attn, w_out, layer, seq):
    tok, d = x.shape
    width = rnn.shape[1]
    tpb = seq // MERGE_TM
    resident = pl.Buffered(1)
    return pl.pallas_call(
        _merge_kernel,
        grid=(tok // MERGE_TM,),
        in_specs=[
            pl.BlockSpec((MERGE_TM, d), lambda i: (i, 0)),
            pl.BlockSpec((MERGE_TM, width), lambda i: (i, 0)),
            pl.BlockSpec((MERGE_TM, width), lambda i: (i, 0)),
            pl.BlockSpec((MERGE_TM, d), lambda i: (i, 0)),
            pl.BlockSpec((MERGE_TM, d), lambda i: (i, 1)),
            pl.BlockSpec((None, 1, d), lambda i: (i // tpb, 0, 5)),
            pl.BlockSpec((None, width, d), lambda i: (layer, 0, 0), pipeline_mode=resident),
            pl.BlockSpec((None, width, d), lambda i: (layer, 0, 0), pipeline_mode=resident),
            pl.BlockSpec((None, d, d), lambda i: (layer, 0, 0), pipeline_mode=resident),
        ],
        out_specs=pl.BlockSpec((MERGE_TM, d), lambda i: (i, 0)),
        out_shape=jax.ShapeDtypeStruct((tok, d), F32),
        compiler_params=_params("arbitrary"),
        name="merge",
    )(x, rnn, attn, z, z, mod_l, w_br_rnn, w_br_attn, w_out)


def kernel(x, c, g_ffn1, w_ffn1_up, w_ffn1_down, g_mix, w_in, conv_w, conv_b, lru_wa, lru_ba, lru_wx,
           lru_bx, lru_lambda, attn_sinks, w_br_rnn, w_br_attn, w_out, g_ffn2, w_ffn2_up, w_ffn2_down,
           w_mod, b_mod, g_final):
    batch, seq, d = x.shape
    depth = w_mod.shape[0]
    lru_w = conv_w.shape[2]
    attn_w = N_HEADS * HEAD_DIM
    kv_w = N_KV_HEADS * HEAD_DIM

    gate_col0 = 2 * lru_w + attn_w + 2 * kv_w
    gates_w = 2 * d
    xr_block = gates_w // lru_w
    gr_block = xr_block + 1
    q_block = (gates_w + 2 * lru_w) // attn_w
    kv_block = (gates_w + 2 * lru_w + attn_w) // (2 * kv_w)
    up_bf, dn_bf = w_ffn1_up[0].astype(BF16), w_ffn1_down[0].astype(BF16)
    wbr, wba, wo = w_br_rnn.astype(BF16), w_br_attn.astype(BF16), w_out.astype(BF16)
    wa_b, wx_b = lru_wa.astype(BF16), lru_wx.astype(BF16)

    vec3 = lambda a: a.reshape(depth, 1, a.shape[-1])
    g1, gm, g2 = vec3(g_ffn1), vec3(g_mix), vec3(g_ffn2)
    cb, ba, bx, lam = vec3(conv_b), vec3(lru_ba), vec3(lru_bx), vec3(lru_lambda)
    gf = g_final.reshape(1, d)

    mod_rows = 2 * SUBLANES
    c_pad = jnp.pad(c, ((0, mod_rows - batch), (0, 0)))
    mod = _modulation(c_pad, w_mod, b_mod)

    xs = x.reshape(batch * seq, d)
    for l in range(depth):
        mod_l = mod[l].reshape(mod_rows, 1, N_MOD * d)
        xs, up_bf, dn_bf = _ffn(xs, g1, l, mod_l, 0, up_bf, dn_bf, seq, gf, False,
                                (w_ffn2_up, w_ffn2_down, l))
        h = _normmod_call(xs, gm, mod_l, 3, l, seq)
        z = _inproj(h, w_in, l, gate_col0)
        rnn = _lru(z, conv_w, cb, wa_b, ba, wx_b, bx, lam, l, batch, seq, xr_block, gr_block)
        attn = _attention(z, attn_sinks, l, batch, seq, q_block, kv_block)
        xs = _merge(xs, rnn, attn, z, mod_l, wbr, wba, wo, l, seq)
        if l + 1 < depth:
            xs, up_bf, dn_bf = _ffn(xs, g2, l, mod_l, 6, up_bf, dn_bf, seq, gf, False,
                                    (w_ffn1_up, w_ffn1_down, l + 1))
        else:
            xs, = _ffn(xs, g2, l, mod_l, 6, up_bf, dn_bf, seq, gf, True, None)
    return xs.reshape(batch, seq, d)
```

```python
import functools

import jax
import jax.numpy as jnp
from jax import lax
from jax.experimental import pallas as pl
from jax.experimental.pallas import tpu as pltpu

F32 = jnp.float32
BF16 = jnp.bfloat16

N_HEADS = 16
N_KV_HEADS = 2
HEAD_DIM = 64
WINDOW = 128
LRU_C = 8.0
FFN_RES = 0.5
EPS = 1e-6
N_MOD = 9
SOFTMAX_SCALE = HEAD_DIM ** -0.5
assert SOFTMAX_SCALE == 2.0 ** -3
MASK_VALUE = -1e30

VMEM_LIMIT_BYTES = 56 * 1024 * 1024
SUBLANES = 8
LANES = 128

MOD_TN = 1024
NORM_TM = 512
NORM_ROWS = 32
FFN_TM = 1024
FFN_TF = 512
FFN_UP_SPLIT = 2
FFN_CAST_STEPS = 64
FFN_CAST_DOWN_ROWS = 128
FFN_XROWS = 128
FFN_DOWN_TN = 512
INPROJ_TM = 1024
INPROJ_TN = 1024
ATTN_TQ = 512
MIXER_TM = 256


def _params(*sem):
    return pltpu.CompilerParams(dimension_semantics=sem, vmem_limit_bytes=VMEM_LIMIT_BYTES)


def _sigmoid(x):
    return 0.5 * jnp.tanh(0.5 * x) + 0.5


def _gelu_tanh(x):
    c = 0.7978845608028654
    return 0.5 * x * (1.0 + jnp.tanh(c * (x + 0.044715 * (x * x * x))))


def _rmsnorm(x, g):
    ms = jnp.mean(x * x, axis=-1, keepdims=True)
    return x * lax.rsqrt(ms + EPS) * g


def _normmod(x, g, shift, scale):
    return _rmsnorm(x, g) * (1.0 + scale) + shift


def _mod_kernel(c_ref, w_ref, b_ref, o_ref):
    c = c_ref[...]
    ca = (c * _sigmoid(c)).astype(BF16)
    o_ref[...] = jnp.dot(ca, w_ref[...].astype(BF16), preferred_element_type=F32) + b_ref[...]


def _modulation(c_pad, w_mod, b_mod):
    depth, d, nd = w_mod.shape
    rows = c_pad.shape[0]
    return pl.pallas_call(
        _mod_kernel,
        grid=(depth, nd // MOD_TN),
        in_specs=[
            pl.BlockSpec((rows, d), lambda l, j: (0, 0)),
            pl.BlockSpec((None, d, MOD_TN), lambda l, j: (l, 0, j)),
            pl.BlockSpec((None, 1, MOD_TN), lambda l, j: (l, 0, j)),
        ],
        out_specs=pl.BlockSpec((None, rows, MOD_TN), lambda l, j: (l, 0, j)),
        out_shape=jax.ShapeDtypeStruct((depth, rows, nd), F32),
        compiler_params=_params("arbitrary", "arbitrary"),
        name="modulation",
    )(c_pad, w_mod, b_mod.reshape(depth, 1, nd))


def _mod_spec(d, batch_of, chunk):
    return pl.BlockSpec((None, 1, d), lambda *idx: (batch_of(*idx), 0, chunk))


def _normmod_kernel(x_ref, g_ref, sh_ref, sc_ref, o_ref):
    g, sh, sc = g_ref[...], sh_ref[...], sc_ref[...]

    def body(k, carry):
        rows = pl.ds(pl.multiple_of(k * NORM_ROWS, NORM_ROWS), NORM_ROWS)
        o_ref[rows, :] = _normmod(x_ref[rows, :], g, sh, sc).astype(o_ref.dtype)
        return carry

    lax.fori_loop(0, x_ref.shape[0] // NORM_ROWS, body, 0, unroll=2)


def _normmod_call(x, g, mod_l, chunk0, layer, seq):
    tok, d = x.shape
    tpb = seq // NORM_TM
    return pl.pallas_call(
        _normmod_kernel,
        grid=(tok // NORM_TM,),
        in_specs=[
            pl.BlockSpec((NORM_TM, d), lambda i: (i, 0)),
            pl.BlockSpec((None, 1, d), lambda i: (layer, 0, 0)),
            _mod_spec(d, lambda i: i // tpb, chunk0),
            _mod_spec(d, lambda i: i // tpb, chunk0 + 1),
        ],
        out_specs=pl.BlockSpec((NORM_TM, d), lambda i: (i, 0)),
        out_shape=jax.ShapeDtypeStruct((tok, d), BF16),
        compiler_params=_params("arbitrary"),
        name="normmod",
    )(x, g, mod_l, mod_l)


def _ffn_kernel(*refs, nf, n_tiles, final_norm, cast_next):
    (xa_ref, g_ref, sh_ref, sc_ref, x_ref, ga_ref, wg_ref, wu_ref, wd_ref, gf_ref), refs = refs[:10], refs[10:]
    if cast_next:
        (cui_ref, cdi_ref, o_ref, cuo_ref, cdo_ref), refs = refs[:5], refs[5:]
    else:
        o_ref, refs = refs[0], refs[1:]
    h_ref, act_ref = refs
    s = pl.program_id(0)
    tm, d = o_ref.shape
    nx = tm // FFN_XROWS
    t = s - nx
    tu = jnp.clip(t, 0, n_tiles * nf - 1)
    iu, fu = tu // nf, tu % nf
    fd = jnp.maximum(t - 1, 0) % nf
    slot = s % 2

    if cast_next:
        @pl.when(s < FFN_CAST_STEPS)
        def _():
            cuo_ref[...] = cui_ref[...].astype(BF16)
            cdo_ref[...] = cdi_ref[...].astype(BF16)

    def norm_ahead(hslot, rchunk):
        r0 = pl.multiple_of(rchunk * FFN_XROWS, FFN_XROWS)
        hn = _normmod(xa_ref[...], g_ref[...], sh_ref[...], sc_ref[...])
        h_ref[hslot, pl.ds(r0, FFN_XROWS), :] = hn.astype(BF16)

    @pl.when(t < 0)
    def _():
        norm_ahead(0, s)

    @pl.when(t >= 0)
    def _():
        @pl.when(t == 0)
        def _():
            act_ref[...] = jnp.zeros_like(act_ref)

        @pl.when(fd == 0)
        def _():
            o_ref[...] = jnp.zeros_like(o_ref)

        @pl.when((fu < nx) & (iu + 1 < n_tiles))
        def _():
            norm_ahead((iu + 1) % 2, fu)

        act_prev = act_ref[1 - slot]
        coef = FFN_RES * ga_ref[...]
        for n in range(d // FFN_DOWN_TN):
            cs = slice(n * FFN_DOWN_TN, (n + 1) * FFN_DOWN_TN)
            o_ref[:, cs] += coef[:, cs] * jnp.dot(act_prev, wd_ref[:, cs], preferred_element_type=F32)

        r0 = pl.multiple_of(jnp.minimum(fd, nx - 1) * FFN_XROWS, FFN_XROWS)
        take = jnp.where(fd < nx, 1.0, 0.0)
        o_ref[pl.ds(r0, FFN_XROWS), :] += take * x_ref[...]

        for m in range(FFN_UP_SPLIT):
            rs = slice(m * tm // FFN_UP_SPLIT, (m + 1) * tm // FFN_UP_SPLIT)
            hm = h_ref[iu % 2, rs, :]
            gate = jnp.dot(hm, wg_ref[...], preferred_element_type=F32)
            up = jnp.dot(hm, wu_ref[...], preferred_element_type=F32)
            act_ref[slot, rs, :] = (gate * _sigmoid(gate) * up).astype(BF16)

        if final_norm:
            @pl.when(fd == nf - 1)
            def _():
                o_ref[...] = _rmsnorm(o_ref[...], gf_ref[...])


def _ffn(x, g, g_layer, mod_l, chunk0, w_up, w_down, seq, g_final, final_norm, cast_src):
    tok, d = x.shape
    dff = w_down.shape[0]
    nf = dff // FFN_TF
    n_tiles = tok // FFN_TM
    n_chunks = n_tiles * nf
    n_steps = FFN_TM // FFN_XROWS + n_chunks + 1
    tpb = seq // FFN_TM
    nx = FFN_TM // FFN_XROWS
    up_chunk = lambda s: jnp.clip(s - nx, 0, n_chunks - 1)
    down_tile = lambda s: jnp.maximum(s - nx - 1, 0) // nf
    down_chunk = lambda s: jnp.maximum(s - nx - 1, 0) % nf
    ahead_tile = lambda s: jnp.where(s < nx, 0, jnp.minimum(up_chunk(s) // nf + 1, n_tiles - 1))
    ahead_rows = lambda s: jnp.where(s < nx, s, jnp.minimum(up_chunk(s) % nf, nx - 1))
    in_specs = [
        pl.BlockSpec((FFN_XROWS, d), lambda s: (ahead_tile(s) * nx + ahead_rows(s), 0)),
        pl.BlockSpec((None, 1, d), lambda s: (g_layer, 0, 0)),
        _mod_spec(d, lambda s: ahead_tile(s) // tpb, chunk0),
        _mod_spec(d, lambda s: ahead_tile(s) // tpb, chunk0 + 1),
        pl.BlockSpec((FFN_XROWS, d), lambda s: (down_tile(s) * nx + jnp.minimum(down_chunk(s), nx - 1), 0)),
        _mod_spec(d, lambda s: down_tile(s) // tpb, chunk0 + 2),
        pl.BlockSpec((d, FFN_TF), lambda s: (0, up_chunk(s) % nf)),
        pl.BlockSpec((d, FFN_TF), lambda s: (0, nf + up_chunk(s) % nf)),
        pl.BlockSpec((FFN_TF, d), lambda s: (down_chunk(s), 0)),
        pl.BlockSpec((1, d), lambda s: (0, 0)),
    ]
    args = [x, g, mod_l, mod_l, x, mod_l, w_up, w_up, w_down, g_final]
    out_specs = [pl.BlockSpec((FFN_TM, d), lambda s: (down_tile(s), 0))]
    out_shape = [jax.ShapeDtypeStruct((tok, d), F32)]
    if cast_src is not None:
        src_up, src_down, src_layer = cast_src
        up_rows, down_rows = d // FFN_CAST_STEPS, FFN_CAST_DOWN_ROWS
        assert FFN_CAST_STEPS <= n_steps and dff // down_rows <= FFN_CAST_STEPS
        up_blk = lambda s: jnp.minimum(s, FFN_CAST_STEPS - 1)
        down_blk = lambda s: jnp.minimum(s, dff // down_rows - 1)
        in_specs += [pl.BlockSpec((None, up_rows, 2 * dff), lambda s: (src_layer, up_blk(s), 0)),
                     pl.BlockSpec((None, down_rows, d), lambda s: (src_layer, down_blk(s), 0))]
        args += [src_up, src_down]
        out_specs += [pl.BlockSpec((up_rows, 2 * dff), lambda s: (up_blk(s), 0)),
                      pl.BlockSpec((down_rows, d), lambda s: (down_blk(s), 0))]
        out_shape += [jax.ShapeDtypeStruct((d, 2 * dff), BF16), jax.ShapeDtypeStruct((dff, d), BF16)]
    kern = functools.partial(_ffn_kernel, nf=nf, n_tiles=n_tiles, final_norm=final_norm,
                             cast_next=cast_src is not None)
    return pl.pallas_call(
        kern,
        grid=(n_steps,),
        in_specs=in_specs,
        out_specs=out_specs,
        out_shape=out_shape,
        scratch_shapes=[pltpu.VMEM((2, FFN_TM, d), BF16), pltpu.VMEM((2, FFN_TM, FFN_TF), BF16)],
        compiler_params=_params("arbitrary"),
        name="ffn",
    )(*args)


INPROJ_SIGMOID_TILES = 4
INPROJ_GELU_TILE = 5
INPROJ_WSPLIT = 4


def _inproj_kernel(h_ref, *refs, n_full, tail):
    w_refs, (o_ref, wb_ref) = refs[:INPROJ_WSPLIT], refs[INPROJ_WSPLIT:]
    n = pl.program_id(0)
    sub = wb_ref.shape[1] // INPROJ_WSPLIT

    @pl.when(pl.program_id(1) == 0)
    def _():
        for k, w_ref in enumerate(w_refs):
            wb_ref[:, k * sub:(k + 1) * sub] = w_ref[...].astype(BF16)

    h = h_ref[...]

    @pl.when(n < INPROJ_SIGMOID_TILES)
    def _():
        o_ref[...] = _sigmoid(jnp.dot(h, wb_ref[...], preferred_element_type=F32))

    @pl.when(n == INPROJ_GELU_TILE)
    def _():
        o_ref[...] = _gelu_tanh(jnp.dot(h, wb_ref[...], preferred_element_type=F32))

    @pl.when((n >= INPROJ_SIGMOID_TILES) & (n != INPROJ_GELU_TILE) & (n < n_full))
    def _():
        o_ref[...] = jnp.dot(h, wb_ref[...], preferred_element_type=F32)

    @pl.when(n == n_full)
    def _():
        o_ref[:, :tail] = jnp.dot(h, wb_ref[:, :tail], preferred_element_type=F32)


def _inproj(h, w_in, layer, gate_col0):
    tok, d = h.shape
    ncol = w_in.shape[2]
    n_full = ncol // INPROJ_TN
    tail = ncol - n_full * INPROJ_TN
    sub = INPROJ_TN // INPROJ_WSPLIT
    gate_blk0 = gate_col0 // sub
    last_blk = ncol // sub - 1

    def w_spec(k):
        def index(n, i):
            base = jnp.where(n < INPROJ_SIGMOID_TILES, gate_blk0 + INPROJ_WSPLIT * n,
                             INPROJ_WSPLIT * (n - INPROJ_SIGMOID_TILES))
            return (layer, 0, jnp.minimum(base + k, last_blk))
        return pl.BlockSpec((None, d, sub), index)

    kern = functools.partial(_inproj_kernel, n_full=n_full, tail=tail)
    return pl.pallas_call(
        kern,
        grid=(n_full + 1, tok // INPROJ_TM),
        in_specs=[pl.BlockSpec((INPROJ_TM, d), lambda n, i: (i, 0))] + [w_spec(k) for k in range(INPROJ_WSPLIT)],
        out_specs=pl.BlockSpec((INPROJ_TM, INPROJ_TN), lambda n, i: (i, n)),
        out_shape=jax.ShapeDtypeStruct((tok, ncol), F32),
        scratch_shapes=[pltpu.VMEM((d, INPROJ_TN), BF16)],
        compiler_params=_params("arbitrary", "arbitrary"),
        name="inproj",
    )(h, *([w_in] * INPROJ_WSPLIT))


def _lru_block(n, xr_ref, gg_ref, cw_ref, cb_ref, wa_ref, ba_ref, wx_ref, bx_ref, lam_ref, ext_ref, h_ref):
    tc = xr_ref.shape[0]
    bw = wa_ref.shape[1]
    sl = slice(n * bw, (n + 1) * bw)
    xr = xr_ref[:, sl]
    ext_ref[SUBLANES:SUBLANES + tc, sl] = xr
    kw = cw_ref.shape[0]
    u = cb_ref[:, sl] + cw_ref[kw - 1:kw, sl] * xr
    for k in range(kw - 1):
        back = kw - 1 - k
        u = u + cw_ref[k:k + 1, sl] * ext_ref[SUBLANES - back:SUBLANES - back + tc, sl]
    ext_ref[0:SUBLANES, sl] = xr[tc - SUBLANES:tc, :]

    nlam = -lam_ref[:, sl]
    softplus = jnp.maximum(nlam, 0.0) + jnp.log1p(jnp.exp(-jnp.abs(nlam)))
    decay = -LRU_C * softplus
    ub = u.astype(BF16)
    r = _sigmoid(jnp.dot(ub, wa_ref[n], preferred_element_type=F32) + ba_ref[:, sl])
    ig = _sigmoid(jnp.dot(ub, wx_ref[n], preferred_element_type=F32) + bx_ref[:, sl])
    log_a = r * decay
    a_all = jnp.exp(log_a)
    b_all = jnp.sqrt(-jnp.tanh(log_a) * (a_all * a_all + 1.0)) * (ig * u)

    row = lax.broadcasted_iota(jnp.int32, (SUBLANES, bw), 0)
    h = h_ref[:, sl]
    out = []
    for k in range(tc // SUBLANES):
        rows = slice(k * SUBLANES, (k + 1) * SUBLANES)
        a, b = a_all[rows, :], b_all[rows, :]
        for dist in (1, 2, 4):
            keep = row >= dist
            b = b + a * jnp.where(keep, pltpu.roll(b, dist, 0), 0.0)
            a = a * jnp.where(keep, pltpu.roll(a, dist, 0), 1.0)
        hs = a * h + b
        out.append(hs)
        h = hs[SUBLANES - 1:SUBLANES, :]
    h_ref[:, sl] = h
    return (gg_ref[:, sl] * jnp.concatenate(out, axis=0)).astype(BF16)


def _attn_kernel(sink_ref, q_ref, kvo_ref, kvp_ref, o_ref, *, layer):
    tq = q_ref.shape[0]
    first_valid_key = jnp.where(pl.program_id(1) == 0, WINDOW, 0)
    pairs = N_HEADS // N_KV_HEADS // 2
    rows = pairs * WINDOW
    lane = lax.broadcasted_iota(jnp.int32, (2 * WINDOW, LANES), 1)
    lo = lane < HEAD_DIM
    qpos = lax.broadcasted_iota(jnp.int32, (rows, 2 * WINDOW), 0) % WINDOW
    kpos = lax.broadcasted_iota(jnp.int32, (rows, 2 * WINDOW), 1)
    band = (kpos > qpos) & (kpos <= qpos + WINDOW)
    pair_of_row = lax.broadcasted_iota(jnp.int32, (rows, 1), 0) // WINDOW
    out_lo = lax.broadcasted_iota(jnp.int32, (rows, LANES), 1) < HEAD_DIM

    for jb in range(tq // WINDOW):
        own = kvo_ref[jb * WINDOW:(jb + 1) * WINDOW, :]
        prev = kvp_ref[...] if jb == 0 else kvo_ref[(jb - 1) * WINDOW:jb * WINDOW, :]
        kv = jnp.concatenate([prev, own], axis=0)
        kk, vv = kv[:, :LANES], kv[:, LANES:]
        kk_rot = pltpu.roll(kk, HEAD_DIM, 1)
        vv_rot = pltpu.roll(vv, HEAD_DIM, 1)
        if jb == 0:
            mask = band & (kpos >= first_valid_key)
        else:
            mask = band
        for kvh in range(N_KV_HEADS):
            if kvh == 0:
                k_top, k_bot, v_top, v_bot = kk, kk_rot, vv, vv_rot
            else:
                k_top, k_bot, v_top, v_bot = kk_rot, kk, vv_rot, vv
            kbd = jnp.concatenate([jnp.where(lo, k_top, 0.0), jnp.where(lo, 0.0, k_bot)], axis=0).astype(BF16)
            vbd = jnp.concatenate([jnp.where(lo, v_top, 0.0), jnp.where(lo, 0.0, v_bot)], axis=0).astype(BF16)
            qs = jnp.concatenate(
                [q_ref[jb * WINDOW:(jb + 1) * WINDOW, (pairs * kvh + p) * LANES:(pairs * kvh + p + 1) * LANES]
                 for p in range(pairs)], axis=0)
            qs = (qs * SOFTMAX_SCALE).astype(BF16)
            s = lax.dot_general(qs, kbd, (((1,), (1,)), ((), ())), preferred_element_type=F32)
            probs, inv = [], []
            for half in range(2):
                sh = jnp.where(mask, s[:, half * 2 * WINDOW:(half + 1) * 2 * WINDOW], MASK_VALUE)
                sink = jnp.zeros((rows, 1), F32)
                for p in range(pairs):
                    head = (N_HEADS // N_KV_HEADS) * kvh + 2 * p + half
                    sink = jnp.where(pair_of_row == p, sink_ref[layer, head], sink)
                m = jnp.maximum(jnp.max(sh, axis=-1, keepdims=True), sink)
                e = jnp.exp(sh - m)
                den = jnp.sum(e, axis=-1, keepdims=True) + jnp.exp(sink - m)
                probs.append(e.astype(BF16))
                inv.append(1.0 / den)
            pv = jnp.dot(jnp.concatenate(probs, axis=1), vbd, preferred_element_type=F32)
            pv = pv * jnp.where(out_lo, inv[0], inv[1])
            for p in range(pairs):
                col = (pairs * kvh + p) * LANES
                o_ref[jb * WINDOW:(jb + 1) * WINDOW, col:col + LANES] = (
                    pv[p * WINDOW:(p + 1) * WINDOW, :].astype(o_ref.dtype))


def _attention(z, sinks, layer, batch, seq, q_block, kv_block):
    tok = z.shape[0]
    width = N_HEADS * HEAD_DIM
    kvw = 2 * N_KV_HEADS * HEAD_DIM
    nt = seq // ATTN_TQ
    per = ATTN_TQ // WINDOW
    kern = functools.partial(_attn_kernel, layer=layer)
    return pl.pallas_call(
        kern,
        grid=(batch, nt),
        in_specs=[
            pl.BlockSpec(memory_space=pltpu.SMEM),
            pl.BlockSpec((ATTN_TQ, width), lambda b, i: (b * nt + i, q_block)),
            pl.BlockSpec((ATTN_TQ, kvw), lambda b, i: (b * nt + i, kv_block)),
            pl.BlockSpec((WINDOW, kvw), lambda b, i: (jnp.maximum((b * nt + i) * per - 1, 0), kv_block)),
        ],
        out_specs=pl.BlockSpec((ATTN_TQ, width), lambda b, i: (b * nt + i, 0)),
        out_shape=jax.ShapeDtypeStruct((tok, width), BF16),
        compiler_params=_params("arbitrary", "arbitrary"),
        name="swattn",
    )(sinks, z, z, z)


def _mixer_kernel(xr_ref, gg_ref, cw_ref, cb_ref, wla_ref, ba_ref, wlx_ref, bx_ref, lam_ref,
                  x_ref, attn_ref, sga_ref, sgb_ref, g2_ref, wr_ref, wa_ref, wo_ref, o_ref,
                  rnn_ref, rnp_ref, mrg_ref, mrp_ref, ext_ref, h_ref, *, n_tiles, tiles_per_seq):
    j = pl.program_id(0)
    d = o_ref.shape[1]
    nb, bw, _ = wla_ref.shape
    mc = d // nb

    @pl.when(j == 0)
    def _():
        rnn_ref[...] = jnp.zeros_like(rnn_ref)
        mrg_ref[...] = jnp.zeros_like(mrg_ref)

    @pl.when(jnp.minimum(j, n_tiles - 1) % tiles_per_seq == 0)
    def _():
        ext_ref[0:SUBLANES, :] = jnp.zeros((SUBLANES, ext_ref.shape[1]), F32)
        h_ref[...] = jnp.zeros_like(h_ref)

    rnp_ref[...] = rnn_ref[...]
    mrp_ref[...] = mrg_ref[...]

    for n in range(nb):
        cs = slice(n * mc, (n + 1) * mc)
        br = jnp.dot(rnp_ref[...], wr_ref[:, cs], preferred_element_type=F32)
        ba = jnp.dot(attn_ref[...], wa_ref[:, cs], preferred_element_type=F32)
        mrg_ref[:, cs] = (sga_ref[:, cs] * br + sgb_ref[:, cs] * ba).astype(BF16)
        proj = jnp.dot(mrp_ref[...], wo_ref[:, cs], preferred_element_type=F32)
        o_ref[:, cs] = x_ref[:, cs] + g2_ref[:, cs] * proj
        rnn_ref[:, n * bw:(n + 1) * bw] = _lru_block(
            n, xr_ref, gg_ref, cw_ref, cb_ref, wla_ref, ba_ref, wlx_ref, bx_ref, lam_ref, ext_ref, h_ref)


def _mixer(x, attn, z, mod_l, conv_w, conv_b, wla, ba, wlx, bx, lam, w_br_rnn, w_br_attn, w_out,
           layer, seq, xr_block, gr_block):
    tok, d = x.shape
    width = conv_w.shape[2]
    n_tiles = tok // MIXER_TM
    tps = seq // MIXER_TM
    resident = pl.Buffered(1)
    rec_tile = lambda j: jnp.minimum(j, n_tiles - 1)
    mrg_tile = lambda j: jnp.clip(j - 1, 0, n_tiles - 1)
    out_tile = lambda j: jnp.maximum(j - 2, 0)
    vec = lambda: pl.BlockSpec((None, 1, width), lambda j: (layer, 0, 0))
    blk = lambda: pl.BlockSpec((None,) + wla.shape[1:], lambda j: (layer, 0, 0, 0))
    kern = functools.partial(_mixer_kernel, n_tiles=n_tiles, tiles_per_seq=tps)
    return pl.pallas_call(
        kern,
        grid=(n_tiles + 2,),
        in_specs=[
            pl.BlockSpec((MIXER_TM, width), lambda j: (rec_tile(j), xr_block)),
            pl.BlockSpec((MIXER_TM, width), lambda j: (rec_tile(j), gr_block)),
            pl.BlockSpec((None,) + conv_w.shape[1:], lambda j: (layer, 0, 0)),
            vec(), blk(), vec(), blk(), vec(), vec(),
            pl.BlockSpec((MIXER_TM, d), lambda j: (out_tile(j), 0)),
            pl.BlockSpec((MIXER_TM, width), lambda j: (mrg_tile(j), 0)),
            pl.BlockSpec((MIXER_TM, d), lambda j: (mrg_tile(j), 0)),
            pl.BlockSpec((MIXER_TM, d), lambda j: (mrg_tile(j), 1)),
            _mod_spec(d, lambda j: out_tile(j) // tps, 5),
            pl.BlockSpec((None, width, d), lambda j: (layer, 0, 0), pipeline_mode=resident),
            pl.BlockSpec((None, width, d), lambda j: (layer, 0, 0), pipeline_mode=resident),
            pl.BlockSpec((None, d, d), lambda j: (layer, 0, 0), pipeline_mode=resident),
        ],
        out_specs=pl.BlockSpec((MIXER_TM, d), lambda j: (out_tile(j), 0)),
        out_shape=jax.ShapeDtypeStruct((tok, d), F32),
        scratch_shapes=[
            pltpu.VMEM((MIXER_TM, width), BF16),
            pltpu.VMEM((MIXER_TM, width), BF16),
            pltpu.VMEM((MIXER_TM, d), BF16),
            pltpu.VMEM((MIXER_TM, d), BF16),
            pltpu.VMEM((MIXER_TM + SUBLANES, width), F32),
            pltpu.VMEM((1, width), F32),
        ],
        compiler_params=_params("arbitrary"),
        name="mixer",
    )(z, z, conv_w, conv_b, wla, ba, wlx, bx, lam, x, attn, z, z, mod_l, w_br_rnn, w_br_attn, w_out)


def kernel(x, c, g_ffn1, w_ffn1_up, w_ffn1_down, g_mix, w_in, conv_w, conv_b, lru_wa, lru_ba, lru_wx,
           lru_bx, lru_lambda, attn_sinks, w_br_rnn, w_br_attn, w_out, g_ffn2, w_ffn2_up, w_ffn2_down,
           w_mod, b_mod, g_final):
    batch, seq, d = x.shape
    depth = w_mod.shape[0]
    lru_w = conv_w.shape[2]
    attn_w = N_HEADS * HEAD_DIM
    kv_w = N_KV_HEADS * HEAD_DIM

    gate_col0 = 2 * lru_w + attn_w + 2 * kv_w
    gates_w = 2 * d
    xr_block = gates_w // lru_w
    gr_block = xr_block + 1
    q_block = (gates_w + 2 * lru_w) // attn_w
    kv_block = (gates_w + 2 * lru_w + attn_w) // (2 * kv_w)
    up_bf, dn_bf = w_ffn1_up[0].astype(BF16), w_ffn1_down[0].astype(BF16)
    wbr, wba, wo = w_br_rnn.astype(BF16), w_br_attn.astype(BF16), w_out.astype(BF16)
    wa_b, wx_b = lru_wa.astype(BF16), lru_wx.astype(BF16)

    vec3 = lambda a: a.reshape(depth, 1, a.shape[-1])
    g1, gm, g2 = vec3(g_ffn1), vec3(g_mix), vec3(g_ffn2)
    cb, ba, bx, lam = vec3(conv_b), vec3(lru_ba), vec3(lru_bx), vec3(lru_lambda)
    gf = g_final.reshape(1, d)

    mod_rows = 2 * SUBLANES
    c_pad = jnp.pad(c, ((0, mod_rows - batch), (0, 0)))
    mod = _modulation(c_pad, w_mod, b_mod)

    xs = x.reshape(batch * seq, d)
    for l in range(depth):
        mod_l = mod[l].reshape(mod_rows, 1, N_MOD * d)
        xs, up_bf, dn_bf = _ffn(xs, g1, l, mod_l, 0, up_bf, dn_bf, seq, gf, False,
                                (w_ffn2_up, w_ffn2_down, l))
        h = _normmod_call(xs, gm, mod_l, 3, l, seq)
        z = _inproj(h, w_in, l, gate_col0)
        attn = _attention(z, attn_sinks, l, batch, seq, q_block, kv_block)
        xs = _mixer(xs, attn, z, mod_l, conv_w, cb, wa_b, ba, wx_b, bx, lam, wbr, wba, wo,
                    l, seq, xr_block, gr_block)
        if l + 1 < depth:
            xs, up_bf, dn_bf = _ffn(xs, g2, l, mod_l, 6, up_bf, dn_bf, seq, gf, False,
                                    (w_ffn1_up, w_ffn1_down, l + 1))
        else:
            xs, = _ffn(xs, g2, l, mod_l, 6, up_bf, dn_bf, seq, gf, True, None)
    return xs.reshape(batch, seq, d)
```

```python
import functools

import jax
import jax.numpy as jnp
from jax import lax
from jax.experimental import pallas as pl
from jax.experimental.pallas import tpu as pltpu

F32 = jnp.float32
BF16 = jnp.bfloat16

N_HEADS = 16
N_KV_HEADS = 2
HEAD_DIM = 64
WINDOW = 128
LRU_C = 8.0
FFN_RES = 0.5
EPS = 1e-6
N_MOD = 9
SOFTMAX_SCALE = HEAD_DIM ** -0.5
assert SOFTMAX_SCALE == 2.0 ** -3
MASK_VALUE = -1e30

VMEM_LIMIT_BYTES = 56 * 1024 * 1024
SUBLANES = 8
LANES = 128

MOD_TN = 1024
NORM_TM = 512
NORM_ROWS = 32
FFN_TM = 1024
FFN_TF = 512
FFN_UP_SPLIT = 2
FFN_CAST_STEPS = 64
FFN_CAST_DOWN_ROWS = 128
FFN_NORM_PIECES = 4
FFN_XROWS = 128
FFN_DOWN_TN = 512
INPROJ_TM = 1024
INPROJ_TN = 1024
ATTN_TQ = 512
MIXER_TM = 256
LRU_ROW_SPLIT = 1


def _params(*sem):
    return pltpu.CompilerParams(dimension_semantics=sem, vmem_limit_bytes=VMEM_LIMIT_BYTES)


def _sigmoid(x):
    return 0.5 * jnp.tanh(0.5 * x) + 0.5


def _gelu_tanh(x):
    c = 0.7978845608028654
    return 0.5 * x * (1.0 + jnp.tanh(c * (x + 0.044715 * (x * x * x))))


def _rmsnorm(x, g):
    ms = jnp.mean(x * x, axis=-1, keepdims=True)
    return x * lax.rsqrt(ms + EPS) * g


def _normmod(x, g, shift, scale):
    return _rmsnorm(x, g) * (1.0 + scale) + shift


def _mod_kernel(c_ref, w_ref, b_ref, cui_ref, cdi_ref, o_ref, cuo_ref, cdo_ref):
    c = c_ref[...]
    ca = (c * _sigmoid(c)).astype(BF16)
    o_ref[...] = jnp.dot(ca, w_ref[...].astype(BF16), preferred_element_type=F32) + b_ref[...]

    @pl.when(pl.program_id(0) * pl.num_programs(1) + pl.program_id(1) < FFN_CAST_STEPS)
    def _():
        cuo_ref[...] = cui_ref[...].astype(BF16)
        cdo_ref[...] = cdi_ref[...].astype(BF16)


def _cast_specs(src_up, src_down, src_layer, step_of):
    _, d, two_dff = src_up.shape
    dff = src_down.shape[1]
    up_rows, down_rows = d // FFN_CAST_STEPS, FFN_CAST_DOWN_ROWS
    assert dff // down_rows <= FFN_CAST_STEPS
    up_blk = lambda *i: jnp.minimum(step_of(*i), FFN_CAST_STEPS - 1)
    down_blk = lambda *i: jnp.minimum(step_of(*i), dff // down_rows - 1)
    in_specs = [pl.BlockSpec((None, up_rows, two_dff), lambda *i: (src_layer, up_blk(*i), 0)),
                pl.BlockSpec((None, down_rows, d), lambda *i: (src_layer, down_blk(*i), 0))]
    out_specs = [pl.BlockSpec((up_rows, two_dff), lambda *i: (up_blk(*i), 0)),
                 pl.BlockSpec((down_rows, d), lambda *i: (down_blk(*i), 0))]
    out_shape = [jax.ShapeDtypeStruct((d, two_dff), BF16), jax.ShapeDtypeStruct((dff, d), BF16)]
    return in_specs, out_specs, out_shape


def _modulation(c_pad, w_mod, b_mod, w_up, w_down):
    depth, d, nd = w_mod.shape
    rows = c_pad.shape[0]
    nj = nd // MOD_TN
    assert depth * nj >= FFN_CAST_STEPS
    cast_in, cast_out, cast_shape = _cast_specs(w_up, w_down, 0, lambda l, j: l * nj + j)
    return pl.pallas_call(
        _mod_kernel,
        grid=(depth, nj),
        in_specs=[
            pl.BlockSpec((rows, d), lambda l, j: (0, 0)),
            pl.BlockSpec((None, d, MOD_TN), lambda l, j: (l, 0, j)),
            pl.BlockSpec((None, 1, MOD_TN), lambda l, j: (l, 0, j)),
        ] + cast_in,
        out_specs=[pl.BlockSpec((None, rows, MOD_TN), lambda l, j: (l, 0, j))] + cast_out,
        out_shape=[jax.ShapeDtypeStruct((depth, rows, nd), F32)] + cast_shape,
        compiler_params=_params("arbitrary", "arbitrary"),
        name="modulation",
    )(c_pad, w_mod, b_mod.reshape(depth, 1, nd), w_up, w_down)


def _mod_spec(d, batch_of, chunk):
    return pl.BlockSpec((None, 1, d), lambda *idx: (batch_of(*idx), 0, chunk))


def _normmod_kernel(x_ref, g_ref, sh_ref, sc_ref, o_ref):
    g, sh, sc = g_ref[...], sh_ref[...], sc_ref[...]

    def body(k, carry):
        rows = pl.ds(pl.multiple_of(k * NORM_ROWS, NORM_ROWS), NORM_ROWS)
        o_ref[rows, :] = _normmod(x_ref[rows, :], g, sh, sc).astype(o_ref.dtype)
        return carry

    lax.fori_loop(0, x_ref.shape[0] // NORM_ROWS, body, 0, unroll=2)


def _normmod_call(x, g, mod_l, chunk0, layer, seq):
    tok, d = x.shape
    tpb = seq // NORM_TM
    return pl.pallas_call(
        _normmod_kernel,
        grid=(tok // NORM_TM,),
        in_specs=[
            pl.BlockSpec((NORM_TM, d), lambda i: (i, 0)),
            pl.BlockSpec((None, 1, d), lambda i: (layer, 0, 0)),
            _mod_spec(d, lambda i: i // tpb, chunk0),
            _mod_spec(d, lambda i: i // tpb, chunk0 + 1),
        ],
        out_specs=pl.BlockSpec((NORM_TM, d), lambda i: (i, 0)),
        out_shape=jax.ShapeDtypeStruct((tok, d), BF16),
        compiler_params=_params("arbitrary"),
        name="normmod",
    )(x, g, mod_l, mod_l)


def _ffn_kernel(*refs, nf, n_tiles, final_norm, cast_next):
    (xa_ref, g_ref, sh_ref, sc_ref, x_ref, ga_ref, wg_ref, wu_ref, wd_ref, gf_ref), refs = refs[:10], refs[10:]
    if cast_next:
        (cui_ref, cdi_ref, o_ref, cuo_ref, cdo_ref), refs = refs[:5], refs[5:]
    else:
        o_ref, refs = refs[0], refs[1:]
    hn_ref, hc_ref, act_ref = refs
    s = pl.program_id(0)
    tm, d = o_ref.shape
    nx = tm // FFN_XROWS
    t = s - nx
    tu = jnp.clip(t, 0, n_tiles * nf - 1)
    iu, fu = tu // nf, tu % nf
    fd = jnp.maximum(t - 1, 0) % nf
    slot = s % 2

    if cast_next:
        @pl.when(s < FFN_CAST_STEPS)
        def _():
            cuo_ref[...] = cui_ref[...].astype(BF16)
            cdo_ref[...] = cdi_ref[...].astype(BF16)

    piece_rows = FFN_XROWS // FFN_NORM_PIECES
    pieces_per_half = FFN_NORM_PIECES // FFN_UP_SPLIT

    def norm_piece(p, rchunk):
        rs = slice(p * piece_rows, (p + 1) * piece_rows)
        r0 = pl.multiple_of(rchunk * FFN_XROWS + p * piece_rows, piece_rows)
        hn = _normmod(xa_ref[rs, :], g_ref[...], sh_ref[...], sc_ref[...])
        hn_ref[pl.ds(r0, piece_rows), :] = hn.astype(BF16)

    @pl.when(t < 0)
    def _():
        for p in range(FFN_NORM_PIECES):
            norm_piece(p, s)

    @pl.when(t >= 0)
    def _():
        @pl.when(t == 0)
        def _():
            act_ref[...] = jnp.zeros_like(act_ref)

        @pl.when(fd == 0)
        def _():
            o_ref[...] = jnp.zeros_like(o_ref)

        @pl.when(fu == 0)
        def _():
            hc_ref[...] = hn_ref[...]

        ahead_chunk = jnp.minimum(fu, nx - 1)

        act_prev = act_ref[1 - slot]
        coef = FFN_RES * ga_ref[...]
        for n in range(d // FFN_DOWN_TN):
            cs = slice(n * FFN_DOWN_TN, (n + 1) * FFN_DOWN_TN)
            o_ref[:, cs] += coef[:, cs] * jnp.dot(act_prev, wd_ref[:, cs], preferred_element_type=F32)

        r0 = pl.multiple_of(jnp.minimum(fd, nx - 1) * FFN_XROWS, FFN_XROWS)
        take = jnp.where(fd < nx, 1.0, 0.0)
        o_ref[pl.ds(r0, FFN_XROWS), :] += take * x_ref[...]

        for m in range(FFN_UP_SPLIT):
            rs = slice(m * tm // FFN_UP_SPLIT, (m + 1) * tm // FFN_UP_SPLIT)
            hm = hc_ref[rs, :]
            gate = jnp.dot(hm, wg_ref[...], preferred_element_type=F32)
            for p in range(m * pieces_per_half, (m + 1) * pieces_per_half):
                norm_piece(p, ahead_chunk)
            up = jnp.dot(hm, wu_ref[...], preferred_element_type=F32)
            act_ref[slot, rs, :] = (gate * _sigmoid(gate) * up).astype(BF16)

        if final_norm:
            @pl.when(fd == nf - 1)
            def _():
                o_ref[...] = _rmsnorm(o_ref[...], gf_ref[...])


def _ffn(x, g, g_layer, mod_l, chunk0, w_up, w_down, seq, g_final, final_norm, cast_src):
    tok, d = x.shape
    dff = w_down.shape[0]
    nf = dff // FFN_TF
    n_tiles = tok // FFN_TM
    n_chunks = n_tiles * nf
    n_steps = FFN_TM // FFN_XROWS + n_chunks + 1
    tpb = seq // FFN_TM
    nx = FFN_TM // FFN_XROWS
    up_chunk = lambda s: jnp.clip(s - nx, 0, n_chunks - 1)
    down_tile = lambda s: jnp.maximum(s - nx - 1, 0) // nf
    down_chunk = lambda s: jnp.maximum(s - nx - 1, 0) % nf
    ahead_tile = lambda s: jnp.where(s < nx, 0, jnp.minimum(up_chunk(s) // nf + 1, n_tiles - 1))
    ahead_rows = lambda s: jnp.where(s < nx, s, jnp.minimum(up_chunk(s) % nf, nx - 1))
    in_specs = [
        pl.BlockSpec((FFN_XROWS, d), lambda s: (ahead_tile(s) * nx + ahead_rows(s), 0)),
        pl.BlockSpec((None, 1, d), lambda s: (g_layer, 0, 0)),
        _mod_spec(d, lambda s: ahead_tile(s) // tpb, chunk0),
        _mod_spec(d, lambda s: ahead_tile(s) // tpb, chunk0 + 1),
        pl.BlockSpec((FFN_XROWS, d), lambda s: (down_tile(s) * nx + jnp.minimum(down_chunk(s), nx - 1), 0)),
        _mod_spec(d, lambda s: down_tile(s) // tpb, chunk0 + 2),
        pl.BlockSpec((d, FFN_TF), lambda s: (0, up_chunk(s) % nf)),
        pl.BlockSpec((d, FFN_TF), lambda s: (0, nf + up_chunk(s) % nf)),
        pl.BlockSpec((FFN_TF, d), lambda s: (down_chunk(s), 0)),
        pl.BlockSpec((1, d), lambda s: (0, 0)),
    ]
    args = [x, g, mod_l, mod_l, x, mod_l, w_up, w_up, w_down, g_final]
    out_specs = [pl.BlockSpec((FFN_TM, d), lambda s: (down_tile(s), 0))]
    out_shape = [jax.ShapeDtypeStruct((tok, d), F32)]
    if cast_src is not None:
        src_up, src_down, src_layer = cast_src
        assert FFN_CAST_STEPS <= n_steps
        cast_in, cast_out, cast_shape = _cast_specs(src_up, src_down, src_layer, lambda s: s)
        in_specs += cast_in
        args += [src_up, src_down]
        out_specs += cast_out
        out_shape += cast_shape
    kern = functools.partial(_ffn_kernel, nf=nf, n_tiles=n_tiles, final_norm=final_norm,
                             cast_next=cast_src is not None)
    return pl.pallas_call(
        kern,
        grid=(n_steps,),
        in_specs=in_specs,
        out_specs=out_specs,
        out_shape=out_shape,
        scratch_shapes=[pltpu.VMEM((FFN_TM, d), BF16), pltpu.VMEM((FFN_TM, d), BF16),
                        pltpu.VMEM((2, FFN_TM, FFN_TF), BF16)],
        compiler_params=_params("arbitrary"),
        name="ffn",
    )(*args)


INPROJ_SIGMOID_TILES = 4
INPROJ_GELU_TILE = 5
INPROJ_WSPLIT = 4


def _inproj_kernel(h_ref, *refs, n_full, tail):
    w_refs, (o_ref, wb_ref) = refs[:INPROJ_WSPLIT], refs[INPROJ_WSPLIT:]
    n = pl.program_id(0)
    sub = wb_ref.shape[1] // INPROJ_WSPLIT

    @pl.when(pl.program_id(1) == 0)
    def _():
        for k, w_ref in enumerate(w_refs):
            wb_ref[:, k * sub:(k + 1) * sub] = w_ref[...].astype(BF16)

    h = h_ref[...]

    def project(fn):
        for k in range(INPROJ_WSPLIT):
            cs = slice(k * sub, (k + 1) * sub)
            o_ref[:, cs] = fn(jnp.dot(h, wb_ref[:, cs], preferred_element_type=F32))

    @pl.when(n < INPROJ_SIGMOID_TILES)
    def _():
        project(_sigmoid)

    @pl.when(n == INPROJ_GELU_TILE)
    def _():
        project(_gelu_tanh)

    @pl.when((n >= INPROJ_SIGMOID_TILES) & (n != INPROJ_GELU_TILE) & (n < n_full))
    def _():
        project(lambda acc: acc)

    @pl.when(n == n_full)
    def _():
        o_ref[:, :tail] = jnp.dot(h, wb_ref[:, :tail], preferred_element_type=F32)


def _inproj(h, w_in, layer, gate_col0):
    tok, d = h.shape
    ncol = w_in.shape[2]
    n_full = ncol // INPROJ_TN
    tail = ncol - n_full * INPROJ_TN
    sub = INPROJ_TN // INPROJ_WSPLIT
    gate_blk0 = gate_col0 // sub
    last_blk = ncol // sub - 1

    def w_spec(k):
        def index(n, i):
            base = jnp.where(n < INPROJ_SIGMOID_TILES, gate_blk0 + INPROJ_WSPLIT * n,
                             INPROJ_WSPLIT * (n - INPROJ_SIGMOID_TILES))
            return (layer, 0, jnp.minimum(base + k, last_blk))
        return pl.BlockSpec((None, d, sub), index)

    kern = functools.partial(_inproj_kernel, n_full=n_full, tail=tail)
    return pl.pallas_call(
        kern,
        grid=(n_full + 1, tok // INPROJ_TM),
        in_specs=[pl.BlockSpec((INPROJ_TM, d), lambda n, i: (i, 0))] + [w_spec(k) for k in range(INPROJ_WSPLIT)],
        out_specs=pl.BlockSpec((INPROJ_TM, INPROJ_TN), lambda n, i: (i, n)),
        out_shape=jax.ShapeDtypeStruct((tok, ncol), F32),
        scratch_shapes=[pltpu.VMEM((d, INPROJ_TN), BF16)],
        compiler_params=_params("arbitrary", "arbitrary"),
        name="inproj",
    )(h, *([w_in] * INPROJ_WSPLIT))


def _lru_block(n, xr_ref, gg_ref, cw_ref, cb_ref, wa_ref, ba_ref, wx_ref, bx_ref, lam_ref, ext_ref, h_ref,
               out_ref):
    tc = xr_ref.shape[0]
    bw = wa_ref.shape[1]
    sl = slice(n * bw, (n + 1) * bw)
    ext_ref[SUBLANES:SUBLANES + tc, sl] = xr_ref[:, sl]
    kw = cw_ref.shape[0]
    nlam = -lam_ref[:, sl]
    softplus = jnp.maximum(nlam, 0.0) + jnp.log1p(jnp.exp(-jnp.abs(nlam)))
    decay = -LRU_C * softplus
    row = lax.broadcasted_iota(jnp.int32, (SUBLANES, bw), 0)
    h = h_ref[:, sl]
    pr = tc // LRU_ROW_SPLIT
    for q in range(LRU_ROW_SPLIT):
        base = SUBLANES + q * pr
        u = cb_ref[:, sl] + cw_ref[kw - 1:kw, sl] * ext_ref[base:base + pr, sl]
        for k in range(kw - 1):
            back = kw - 1 - k
            u = u + cw_ref[k:k + 1, sl] * ext_ref[base - back:base - back + pr, sl]
        ub = u.astype(BF16)
        r = _sigmoid(jnp.dot(ub, wa_ref[n], preferred_element_type=F32) + ba_ref[:, sl])
        ig = _sigmoid(jnp.dot(ub, wx_ref[n], preferred_element_type=F32) + bx_ref[:, sl])
        log_a = r * decay
        a_all = jnp.exp(log_a)
        b_all = jnp.sqrt(-jnp.tanh(log_a) * (a_all * a_all + 1.0)) * (ig * u)

        out = []
        for k in range(pr // SUBLANES):
            rows = slice(k * SUBLANES, (k + 1) * SUBLANES)
            a, b = a_all[rows, :], b_all[rows, :]
            for dist in (1, 2, 4):
                keep = row >= dist
                b = b + a * jnp.where(keep, pltpu.roll(b, dist, 0), 0.0)
                a = a * jnp.where(keep, pltpu.roll(a, dist, 0), 1.0)
            hs = a * h + b
            out.append(hs)
            h = hs[SUBLANES - 1:SUBLANES, :]
        rows = slice(q * pr, (q + 1) * pr)
        out_ref[rows, sl] = (gg_ref[rows, sl] * jnp.concatenate(out, axis=0)).astype(BF16)
    h_ref[:, sl] = h
    ext_ref[0:SUBLANES, sl] = ext_ref[tc:tc + SUBLANES, sl]


def _attn_kernel(sink_ref, q_ref, kvo_ref, kvp_ref, o_ref, *, layer):
    tq = q_ref.shape[0]
    first_valid_key = jnp.where(pl.program_id(1) == 0, WINDOW, 0)
    pairs = N_HEADS // N_KV_HEADS // 2
    rows = pairs * WINDOW
    lane = lax.broadcasted_iota(jnp.int32, (2 * WINDOW, LANES), 1)
    lo = lane < HEAD_DIM
    qpos = lax.broadcasted_iota(jnp.int32, (rows, 2 * WINDOW), 0) % WINDOW
    kpos = lax.broadcasted_iota(jnp.int32, (rows, 2 * WINDOW), 1)
    band = (kpos > qpos) & (kpos <= qpos + WINDOW)
    pair_of_row = lax.broadcasted_iota(jnp.int32, (rows, 1), 0) // WINDOW
    out_lo = lax.broadcasted_iota(jnp.int32, (rows, LANES), 1) < HEAD_DIM

    for jb in range(tq // WINDOW):
        own = kvo_ref[jb * WINDOW:(jb + 1) * WINDOW, :]
        prev = kvp_ref[...] if jb == 0 else kvo_ref[(jb - 1) * WINDOW:jb * WINDOW, :]
        kv = jnp.concatenate([prev, own], axis=0)
        kk, vv = kv[:, :LANES], kv[:, LANES:]
        kk_rot = pltpu.roll(kk, HEAD_DIM, 1)
        vv_rot = pltpu.roll(vv, HEAD_DIM, 1)
        if jb == 0:
            mask = band & (kpos >= first_valid_key)
        else:
            mask = band
        for kvh in range(N_KV_HEADS):
            if kvh == 0:
                k_top, k_bot, v_top, v_bot = kk, kk_rot, vv, vv_rot
            else:
                k_top, k_bot, v_top, v_bot = kk_rot, kk, vv_rot, vv
            kbd = jnp.concatenate([jnp.where(lo, k_top, 0.0), jnp.where(lo, 0.0, k_bot)], axis=0).astype(BF16)
            vbd = jnp.concatenate([jnp.where(lo, v_top, 0.0), jnp.where(lo, 0.0, v_bot)], axis=0).astype(BF16)
            qs = jnp.concatenate(
                [q_ref[jb * WINDOW:(jb + 1) * WINDOW, (pairs * kvh + p) * LANES:(pairs * kvh + p + 1) * LANES]
                 for p in range(pairs)], axis=0)
            qs = (qs * SOFTMAX_SCALE).astype(BF16)
            s = lax.dot_general(qs, kbd, (((1,), (1,)), ((), ())), preferred_element_type=F32)
            probs, inv = [], []
            for half in range(2):
                sh = jnp.where(mask, s[:, half * 2 * WINDOW:(half + 1) * 2 * WINDOW], MASK_VALUE)
                sink = jnp.zeros((rows, 1), F32)
                for p in range(pairs):
                    head = (N_HEADS // N_KV_HEADS) * kvh + 2 * p + half
                    sink = jnp.where(pair_of_row == p, sink_ref[layer, head], sink)
                m = jnp.maximum(jnp.max(sh, axis=-1, keepdims=True), sink)
                e = jnp.exp(sh - m)
                den = jnp.sum(e, axis=-1, keepdims=True) + jnp.exp(sink - m)
                probs.append(e.astype(BF16))
                inv.append(1.0 / den)
            pv = jnp.dot(jnp.concatenate(probs, axis=1), vbd, preferred_element_type=F32)
            pv = pv * jnp.where(out_lo, inv[0], inv[1])
            for p in range(pairs):
                col = (pairs * kvh + p) * LANES
                o_ref[jb * WINDOW:(jb + 1) * WINDOW, col:col + LANES] = (
                    pv[p * WINDOW:(p + 1) * WINDOW, :].astype(o_ref.dtype))


def _attention(z, sinks, layer, batch, seq, q_block, kv_block):
    tok = z.shape[0]
    width = N_HEADS * HEAD_DIM
    kvw = 2 * N_KV_HEADS * HEAD_DIM
    nt = seq // ATTN_TQ
    per = ATTN_TQ // WINDOW
    kern = functools.partial(_attn_kernel, layer=layer)
    return pl.pallas_call(
        kern,
        grid=(batch, nt),
        in_specs=[
            pl.BlockSpec(memory_space=pltpu.SMEM),
            pl.BlockSpec((ATTN_TQ, width), lambda b, i: (b * nt + i, q_block)),
            pl.BlockSpec((ATTN_TQ, kvw), lambda b, i: (b * nt + i, kv_block)),
            pl.BlockSpec((WINDOW, kvw), lambda b, i: (jnp.maximum((b * nt + i) * per - 1, 0), kv_block)),
        ],
        out_specs=pl.BlockSpec((ATTN_TQ, width), lambda b, i: (b * nt + i, 0)),
        out_shape=jax.ShapeDtypeStruct((tok, width), BF16),
        compiler_params=_params("arbitrary", "arbitrary"),
        name="swattn",
    )(sinks, z, z, z)


def _mixer_kernel(xr_ref, gg_ref, cw_ref, cb_ref, wla_ref, ba_ref, wlx_ref, bx_ref, lam_ref,
                  x_ref, attn_ref, sga_ref, sgb_ref, g2_ref, wr_ref, wa_ref, wo_ref, o_ref,
                  rnn_ref, rnp_ref, mrg_ref, mrp_ref, ext_ref, h_ref, *, n_tiles, tiles_per_seq):
    j = pl.program_id(0)
    d = o_ref.shape[1]
    nb, bw, _ = wla_ref.shape
    mc = d // nb

    @pl.when(j == 0)
    def _():
        rnn_ref[...] = jnp.zeros_like(rnn_ref)
        mrg_ref[...] = jnp.zeros_like(mrg_ref)

    @pl.when(jnp.minimum(j, n_tiles - 1) % tiles_per_seq == 0)
    def _():
        ext_ref[0:SUBLANES, :] = jnp.zeros((SUBLANES, ext_ref.shape[1]), F32)
        h_ref[...] = jnp.zeros_like(h_ref)

    rnp_ref[...] = rnn_ref[...]
    mrp_ref[...] = mrg_ref[...]

    for n in range(nb):
        cs = slice(n * mc, (n + 1) * mc)
        br = jnp.dot(rnp_ref[...], wr_ref[:, cs], preferred_element_type=F32)
        ba = jnp.dot(attn_ref[...], wa_ref[:, cs], preferred_element_type=F32)
        mrg_ref[:, cs] = (sga_ref[:, cs] * br + sgb_ref[:, cs] * ba).astype(BF16)
        proj = jnp.dot(mrp_ref[...], wo_ref[:, cs], preferred_element_type=F32)
        o_ref[:, cs] = x_ref[:, cs] + g2_ref[:, cs] * proj
        _lru_block(n, xr_ref, gg_ref, cw_ref, cb_ref, wla_ref, ba_ref, wlx_ref, bx_ref, lam_ref, ext_ref, h_ref,
                   rnn_ref)


def _mixer(x, attn, z, mod_l, conv_w, conv_b, wla, ba, wlx, bx, lam, w_br_rnn, w_br_attn, w_out,
           layer, seq, xr_block, gr_block):
    tok, d = x.shape
    width = conv_w.shape[2]
    n_tiles = tok // MIXER_TM
    tps = seq // MIXER_TM
    resident = pl.Buffered(1)
    rec_tile = lambda j: jnp.minimum(j, n_tiles - 1)
    mrg_tile = lambda j: jnp.clip(j - 1, 0, n_tiles - 1)
    out_tile = lambda j: jnp.maximum(j - 2, 0)
    vec = lambda: pl.BlockSpec((None, 1, width), lambda j: (layer, 0, 0))
    blk = lambda: pl.BlockSpec((None,) + wla.shape[1:], lambda j: (layer, 0, 0, 0))
    kern = functools.partial(_mixer_kernel, n_tiles=n_tiles, tiles_per_seq=tps)
    return pl.pallas_call(
        kern,
        grid=(n_tiles + 2,),
        in_specs=[
            pl.BlockSpec((MIXER_TM, width), lambda j: (rec_tile(j), xr_block)),
            pl.BlockSpec((MIXER_TM, width), lambda j: (rec_tile(j), gr_block)),
            pl.BlockSpec((None,) + conv_w.shape[1:], lambda j: (layer, 0, 0)),
            vec(), blk(), vec(), blk(), vec(), vec(),
            pl.BlockSpec((MIXER_TM, d), lambda j: (out_tile(j), 0)),
            pl.BlockSpec((MIXER_TM, width), lambda j: (mrg_tile(j), 0)),
            pl.BlockSpec((MIXER_TM, d), lambda j: (mrg_tile(j), 0)),
            pl.BlockSpec((MIXER_TM, d), lambda j: (mrg_tile(j), 1)),
            _mod_spec(d, lambda j: out_tile(j) // tps, 5),
            pl.BlockSpec((None, width, d), lambda j: (layer, 0, 0), pipeline_mode=resident),
            pl.BlockSpec((None, width, d), lambda j: (layer, 0, 0), pipeline_mode=resident),
            pl.BlockSpec((None, d, d), lambda j: (layer, 0, 0), pipeline_mode=resident),
        ],
        out_specs=pl.BlockSpec((MIXER_TM, d), lambda j: (out_tile(j), 0)),
        out_shape=jax.ShapeDtypeStruct((tok, d), F32),
        scratch_shapes=[
            pltpu.VMEM((MIXER_TM, width), BF16),
            pltpu.VMEM((MIXER_TM, width), BF16),
            pltpu.VMEM((MIXER_TM, d), BF16),
            pltpu.VMEM((MIXER_TM, d), BF16),
            pltpu.VMEM((MIXER_TM + SUBLANES, width), F32),
            pltpu.VMEM((1, width), F32),
        ],
        compiler_params=_params("arbitrary"),
        name="mixer",
    )(z, z, conv_w, conv_b, wla, ba, wlx, bx, lam, x, attn, z, z, mod_l, w_br_rnn, w_br_attn, w_out)


def kernel(x, c, g_ffn1, w_ffn1_up, w_ffn1_down, g_mix, w_in, conv_w, conv_b, lru_wa, lru_ba, lru_wx,
           lru_bx, lru_lambda, attn_sinks, w_br_rnn, w_br_attn, w_out, g_ffn2, w_ffn2_up, w_ffn2_down,
           w_mod, b_mod, g_final):
    batch, seq, d = x.shape
    depth = w_mod.shape[0]
    lru_w = conv_w.shape[2]
    attn_w = N_HEADS * HEAD_DIM
    kv_w = N_KV_HEADS * HEAD_DIM

    gate_col0 = 2 * lru_w + attn_w + 2 * kv_w
    gates_w = 2 * d
    xr_block = gates_w // lru_w
    gr_block = xr_block + 1
    q_block = (gates_w + 2 * lru_w) // attn_w
    kv_block = (gates_w + 2 * lru_w + attn_w) // (2 * kv_w)
    wbr, wba, wo = w_br_rnn.astype(BF16), w_br_attn.astype(BF16), w_out.astype(BF16)
    wa_b, wx_b = lru_wa.astype(BF16), lru_wx.astype(BF16)

    vec3 = lambda a: a.reshape(depth, 1, a.shape[-1])
    g1, gm, g2 = vec3(g_ffn1), vec3(g_mix), vec3(g_ffn2)
    cb, ba, bx, lam = vec3(conv_b), vec3(lru_ba), vec3(lru_bx), vec3(lru_lambda)
    gf = g_final.reshape(1, d)

    mod_rows = 2 * SUBLANES
    c_pad = jnp.pad(c, ((0, mod_rows - batch), (0, 0)))
    mod, up_bf, dn_bf = _modulation(c_pad, w_mod, b_mod, w_ffn1_up, w_ffn1_down)

    xs = x.reshape(batch * seq, d)
    for l in range(depth):
        mod_l = mod[l].reshape(mod_rows, 1, N_MOD * d)
        xs, up_bf, dn_bf = _ffn(xs, g1, l, mod_l, 0, up_bf, dn_bf, seq, gf, False,
                                (w_ffn2_up, w_ffn2_down, l))
        h = _normmod_call(xs, gm, mod_l, 3, l, seq)
        z = _inproj(h, w_in, l, gate_col0)
        attn = _attention(z, attn_sinks, l, batch, seq, q_block, kv_block)
        xs = _mixer(xs, attn, z, mod_l, conv_w, cb, wa_b, ba, wx_b, bx, lam, wbr, wba, wo,
                    l, seq, xr_block, gr_block)
        if l + 1 < depth:
            xs, up_bf, dn_bf = _ffn(xs, g2, l, mod_l, 6, up_bf, dn_bf, seq, gf, False,
                                    (w_ffn1_up, w_ffn1_down, l + 1))
        else:
            xs, = _ffn(xs, g2, l, mod_l, 6, up_bf, dn_bf, seq, gf, True, None)
    return xs.reshape(batch, seq, d)
```

```python
import functools

import jax
import jax.numpy as jnp
from jax import lax
from jax.experimental import pallas as pl
from jax.experimental.pallas import tpu as pltpu

F32 = jnp.float32
BF16 = jnp.bfloat16

N_HEADS = 16
N_KV_HEADS = 2
HEAD_DIM = 64
WINDOW = 128
LRU_C = 8.0
FFN_RES = 0.5
EPS = 1e-6
N_MOD = 9
SOFTMAX_SCALE = HEAD_DIM ** -0.5
assert SOFTMAX_SCALE == 2.0 ** -3
MASK_VALUE = -1e30

VMEM_LIMIT_BYTES = 56 * 1024 * 1024
SUBLANES = 8
LANES = 128

MOD_TN = 1024
NORM_TM = 512
NORM_ROWS = 32
FFN_TM = 1024
FFN_TF = 512
FFN_UP_SPLIT = 2
FFN_CAST_STEPS = 64
FFN_CAST_DOWN_ROWS = 128
FFN_NORM_PIECES = 4
FFN_XROWS = 128
FFN_DOWN_TN = 512
INPROJ_TM = 1024
INPROJ_TN = 1024
ATTN_TQ = 512
MIXER_TM = 256
MIXER_TN = 256
LRU_ROW_SPLIT = 1


def _params(*sem):
    return pltpu.CompilerParams(dimension_semantics=sem, vmem_limit_bytes=VMEM_LIMIT_BYTES)


def _sigmoid(x):
    return 0.5 * jnp.tanh(0.5 * x) + 0.5


def _gelu_tanh(x):
    c = 0.7978845608028654
    return 0.5 * x * (1.0 + jnp.tanh(c * (x + 0.044715 * (x * x * x))))


def _rmsnorm(x, g):
    ms = jnp.mean(x * x, axis=-1, keepdims=True)
    return x * lax.rsqrt(ms + EPS) * g


def _normmod(x, g, shift, scale):
    return _rmsnorm(x, g) * (1.0 + scale) + shift


def _mod_kernel(c_ref, w_ref, b_ref, cui_ref, cdi_ref, o_ref, cuo_ref, cdo_ref):
    c = c_ref[...]
    ca = (c * _sigmoid(c)).astype(BF16)
    o_ref[...] = jnp.dot(ca, w_ref[...].astype(BF16), preferred_element_type=F32) + b_ref[...]

    @pl.when(pl.program_id(0) * pl.num_programs(1) + pl.program_id(1) < FFN_CAST_STEPS)
    def _():
        cuo_ref[...] = cui_ref[...].astype(BF16)
        cdo_ref[...] = cdi_ref[...].astype(BF16)


def _cast_specs(src_up, src_down, src_layer, step_of):
    _, d, two_dff = src_up.shape
    dff = src_down.shape[1]
    up_rows, down_rows = d // FFN_CAST_STEPS, FFN_CAST_DOWN_ROWS
    assert dff // down_rows <= FFN_CAST_STEPS
    up_blk = lambda *i: jnp.minimum(step_of(*i), FFN_CAST_STEPS - 1)
    down_blk = lambda *i: jnp.minimum(step_of(*i), dff // down_rows - 1)
    in_specs = [pl.BlockSpec((None, up_rows, two_dff), lambda *i: (src_layer, up_blk(*i), 0)),
                pl.BlockSpec((None, down_rows, d), lambda *i: (src_layer, down_blk(*i), 0))]
    out_specs = [pl.BlockSpec((up_rows, two_dff), lambda *i: (up_blk(*i), 0)),
                 pl.BlockSpec((down_rows, d), lambda *i: (down_blk(*i), 0))]
    out_shape = [jax.ShapeDtypeStruct((d, two_dff), BF16), jax.ShapeDtypeStruct((dff, d), BF16)]
    return in_specs, out_specs, out_shape


def _modulation(c_pad, w_mod, b_mod, w_up, w_down):
    depth, d, nd = w_mod.shape
    rows = c_pad.shape[0]
    nj = nd // MOD_TN
    assert depth * nj >= FFN_CAST_STEPS
    cast_in, cast_out, cast_shape = _cast_specs(w_up, w_down, 0, lambda l, j: l * nj + j)
    return pl.pallas_call(
        _mod_kernel,
        grid=(depth, nj),
        in_specs=[
            pl.BlockSpec((rows, d), lambda l, j: (0, 0)),
            pl.BlockSpec((None, d, MOD_TN), lambda l, j: (l, 0, j)),
            pl.BlockSpec((None, 1, MOD_TN), lambda l, j: (l, 0, j)),
        ] + cast_in,
        out_specs=[pl.BlockSpec((None, rows, MOD_TN), lambda l, j: (l, 0, j))] + cast_out,
        out_shape=[jax.ShapeDtypeStruct((depth, rows, nd), F32)] + cast_shape,
        compiler_params=_params("arbitrary", "arbitrary"),
        name="modulation",
    )(c_pad, w_mod, b_mod.reshape(depth, 1, nd), w_up, w_down)


def _mod_spec(d, batch_of, chunk):
    return pl.BlockSpec((None, 1, d), lambda *idx: (batch_of(*idx), 0, chunk))


def _normmod_kernel(x_ref, g_ref, sh_ref, sc_ref, o_ref):
    g, sh, sc = g_ref[...], sh_ref[...], sc_ref[...]

    def body(k, carry):
        rows = pl.ds(pl.multiple_of(k * NORM_ROWS, NORM_ROWS), NORM_ROWS)
        o_ref[rows, :] = _normmod(x_ref[rows, :], g, sh, sc).astype(o_ref.dtype)
        return carry

    lax.fori_loop(0, x_ref.shape[0] // NORM_ROWS, body, 0, unroll=2)


def _normmod_call(x, g, mod_l, chunk0, layer, seq):
    tok, d = x.shape
    tpb = seq // NORM_TM
    return pl.pallas_call(
        _normmod_kernel,
        grid=(tok // NORM_TM,),
        in_specs=[
            pl.BlockSpec((NORM_TM, d), lambda i: (i, 0)),
            pl.BlockSpec((None, 1, d), lambda i: (layer, 0, 0)),
            _mod_spec(d, lambda i: i // tpb, chunk0),
            _mod_spec(d, lambda i: i // tpb, chunk0 + 1),
        ],
        out_specs=pl.BlockSpec((NORM_TM, d), lambda i: (i, 0)),
        out_shape=jax.ShapeDtypeStruct((tok, d), BF16),
        compiler_params=_params("arbitrary"),
        name="normmod",
    )(x, g, mod_l, mod_l)


def _ffn_kernel(*refs, nf, n_tiles, final_norm, cast_next):
    (xa_ref, g_ref, sh_ref, sc_ref, x_ref, ga_ref, wg_ref, wu_ref, wd_ref, gf_ref), refs = refs[:10], refs[10:]
    if cast_next:
        (cui_ref, cdi_ref, o_ref, cuo_ref, cdo_ref), refs = refs[:5], refs[5:]
    else:
        o_ref, refs = refs[0], refs[1:]
    hn_ref, hc_ref, act_ref = refs
    s = pl.program_id(0)
    tm, d = o_ref.shape
    nx = tm // FFN_XROWS
    t = s - nx
    tu = jnp.clip(t, 0, n_tiles * nf - 1)
    iu, fu = tu // nf, tu % nf
    fd = jnp.maximum(t - 1, 0) % nf
    slot = s % 2

    if cast_next:
        @pl.when(s < FFN_CAST_STEPS)
        def _():
            cuo_ref[...] = cui_ref[...].astype(BF16)
            cdo_ref[...] = cdi_ref[...].astype(BF16)

    piece_rows = FFN_XROWS // FFN_NORM_PIECES
    pieces_per_half = FFN_NORM_PIECES // FFN_UP_SPLIT

    def norm_piece(p, rchunk):
        rs = slice(p * piece_rows, (p + 1) * piece_rows)
        r0 = pl.multiple_of(rchunk * FFN_XROWS + p * piece_rows, piece_rows)
        hn = _normmod(xa_ref[rs, :], g_ref[...], sh_ref[...], sc_ref[...])
        hn_ref[pl.ds(r0, piece_rows), :] = hn.astype(BF16)

    @pl.when(t < 0)
    def _():
        for p in range(FFN_NORM_PIECES):
            norm_piece(p, s)

    @pl.when(t >= 0)
    def _():
        @pl.when(t == 0)
        def _():
            act_ref[...] = jnp.zeros_like(act_ref)

        @pl.when(fd == 0)
        def _():
            o_ref[...] = jnp.zeros_like(o_ref)

        @pl.when(fu == 0)
        def _():
            hc_ref[...] = hn_ref[...]

        ahead_chunk = jnp.minimum(fu, nx - 1)

        act_prev = act_ref[1 - slot]
        coef = FFN_RES * ga_ref[...]
        for n in range(d // FFN_DOWN_TN):
            cs = slice(n * FFN_DOWN_TN, (n + 1) * FFN_DOWN_TN)
            o_ref[:, cs] += coef[:, cs] * jnp.dot(act_prev, wd_ref[:, cs], preferred_element_type=F32)

        r0 = pl.multiple_of(jnp.minimum(fd, nx - 1) * FFN_XROWS, FFN_XROWS)
        take = jnp.where(fd < nx, 1.0, 0.0)
        o_ref[pl.ds(r0, FFN_XROWS), :] += take * x_ref[...]

        for m in range(FFN_UP_SPLIT):
            rs = slice(m * tm // FFN_UP_SPLIT, (m + 1) * tm // FFN_UP_SPLIT)
            hm = hc_ref[rs, :]
            gate = jnp.dot(hm, wg_ref[...], preferred_element_type=F32)
            for p in range(m * pieces_per_half, (m + 1) * pieces_per_half):
                norm_piece(p, ahead_chunk)
            up = jnp.dot(hm, wu_ref[...], preferred_element_type=F32)
            act_ref[slot, rs, :] = (gate * _sigmoid(gate) * up).astype(BF16)

        if final_norm:
            @pl.when(fd == nf - 1)
            def _():
                o_ref[...] = _rmsnorm(o_ref[...], gf_ref[...])


def _ffn(x, g, g_layer, mod_l, chunk0, w_up, w_down, seq, g_final, final_norm, cast_src):
    tok, d = x.shape
    dff = w_down.shape[0]
    nf = dff // FFN_TF
    n_tiles = tok // FFN_TM
    n_chunks = n_tiles * nf
    n_steps = FFN_TM // FFN_XROWS + n_chunks + 1
    tpb = seq // FFN_TM
    nx = FFN_TM // FFN_XROWS
    up_chunk = lambda s: jnp.clip(s - nx, 0, n_chunks - 1)
    down_tile = lambda s: jnp.maximum(s - nx - 1, 0) // nf
    down_chunk = lambda s: jnp.maximum(s - nx - 1, 0) % nf
    ahead_tile = lambda s: jnp.where(s < nx, 0, jnp.minimum(up_chunk(s) // nf + 1, n_tiles - 1))
    ahead_rows = lambda s: jnp.where(s < nx, s, jnp.minimum(up_chunk(s) % nf, nx - 1))
    in_specs = [
        pl.BlockSpec((FFN_XROWS, d), lambda s: (ahead_tile(s) * nx + ahead_rows(s), 0)),
        pl.BlockSpec((None, 1, d), lambda s: (g_layer, 0, 0)),
        _mod_spec(d, lambda s: ahead_tile(s) // tpb, chunk0),
        _mod_spec(d, lambda s: ahead_tile(s) // tpb, chunk0 + 1),
        pl.BlockSpec((FFN_XROWS, d), lambda s: (down_tile(s) * nx + jnp.minimum(down_chunk(s), nx - 1), 0)),
        _mod_spec(d, lambda s: down_tile(s) // tpb, chunk0 + 2),
        pl.BlockSpec((d, FFN_TF), lambda s: (0, up_chunk(s) % nf)),
        pl.BlockSpec((d, FFN_TF), lambda s: (0, nf + up_chunk(s) % nf)),
        pl.BlockSpec((FFN_TF, d), lambda s: (down_chunk(s), 0)),
        pl.BlockSpec((1, d), lambda s: (0, 0)),
    ]
    args = [x, g, mod_l, mod_l, x, mod_l, w_up, w_up, w_down, g_final]
    out_specs = [pl.BlockSpec((FFN_TM, d), lambda s: (down_tile(s), 0))]
    out_shape = [jax.ShapeDtypeStruct((tok, d), F32)]
    if cast_src is not None:
        src_up, src_down, src_layer = cast_src
        assert FFN_CAST_STEPS <= n_steps
        cast_in, cast_out, cast_shape = _cast_specs(src_up, src_down, src_layer, lambda s: s)
        in_specs += cast_in
        args += [src_up, src_down]
        out_specs += cast_out
        out_shape += cast_shape
    kern = functools.partial(_ffn_kernel, nf=nf, n_tiles=n_tiles, final_norm=final_norm,
                             cast_next=cast_src is not None)
    return pl.pallas_call(
        kern,
        grid=(n_steps,),
        in_specs=in_specs,
        out_specs=out_specs,
        out_shape=out_shape,
        scratch_shapes=[pltpu.VMEM((FFN_TM, d), BF16), pltpu.VMEM((FFN_TM, d), BF16),
                        pltpu.VMEM((2, FFN_TM, FFN_TF), BF16)],
        compiler_params=_params("arbitrary"),
        name="ffn",
    )(*args)


INPROJ_SIGMOID_TILES = 4
INPROJ_GELU_TILE = 5
INPROJ_WSPLIT = 4
INPROJ_EPILOGUE_CHUNKS = 2


def _inproj_kernel(h_ref, *refs, n_full, tail):
    w_refs, (o_ref, wb_ref) = refs[:INPROJ_WSPLIT], refs[INPROJ_WSPLIT:]
    n = pl.program_id(0)
    sub = wb_ref.shape[1] // INPROJ_WSPLIT

    @pl.when(pl.program_id(1) == 0)
    def _():
        for k, w_ref in enumerate(w_refs):
            wb_ref[:, k * sub:(k + 1) * sub] = w_ref[...].astype(BF16)

    h = h_ref[...]

    def project(fn):
        cw = wb_ref.shape[1] // INPROJ_EPILOGUE_CHUNKS
        for k in range(INPROJ_EPILOGUE_CHUNKS):
            cs = slice(k * cw, (k + 1) * cw)
            o_ref[:, cs] = fn(jnp.dot(h, wb_ref[:, cs], preferred_element_type=F32)).astype(o_ref.dtype)

    @pl.when(n < INPROJ_SIGMOID_TILES)
    def _():
        project(_sigmoid)

    @pl.when(n == INPROJ_GELU_TILE)
    def _():
        project(_gelu_tanh)

    @pl.when((n >= INPROJ_SIGMOID_TILES) & (n != INPROJ_GELU_TILE) & (n < n_full))
    def _():
        project(lambda acc: acc)

    @pl.when(n == n_full)
    def _():
        o_ref[:, :tail] = jnp.dot(h, wb_ref[:, :tail], preferred_element_type=F32).astype(o_ref.dtype)


def _inproj(h, w_in, layer, gate_col0):
    tok, d = h.shape
    ncol = w_in.shape[2]
    n_full = ncol // INPROJ_TN
    tail = ncol - n_full * INPROJ_TN
    sub = INPROJ_TN // INPROJ_WSPLIT
    gate_blk0 = gate_col0 // sub
    last_blk = ncol // sub - 1

    def w_spec(k):
        def index(n, i):
            base = jnp.where(n < INPROJ_SIGMOID_TILES, gate_blk0 + INPROJ_WSPLIT * n,
                             INPROJ_WSPLIT * (n - INPROJ_SIGMOID_TILES))
            return (layer, 0, jnp.minimum(base + k, last_blk))
        return pl.BlockSpec((None, d, sub), index)

    kern = functools.partial(_inproj_kernel, n_full=n_full, tail=tail)
    return pl.pallas_call(
        kern,
        grid=(n_full + 1, tok // INPROJ_TM),
        in_specs=[pl.BlockSpec((INPROJ_TM, d), lambda n, i: (i, 0))] + [w_spec(k) for k in range(INPROJ_WSPLIT)],
        out_specs=pl.BlockSpec((INPROJ_TM, INPROJ_TN), lambda n, i: (i, n)),
        out_shape=jax.ShapeDtypeStruct((tok, ncol), BF16),
        scratch_shapes=[pltpu.VMEM((d, INPROJ_TN), BF16)],
        compiler_params=_params("arbitrary", "arbitrary"),
        name="inproj",
    )(h, *([w_in] * INPROJ_WSPLIT))


def _lru_block(n, xr_ref, gg_ref, cw_ref, cb_ref, wa_ref, ba_ref, wx_ref, bx_ref, lam_ref, ext_ref, h_ref,
               out_ref):
    tc = xr_ref.shape[0]
    bw = wa_ref.shape[1]
    sl = slice(n * bw, (n + 1) * bw)
    ext_ref[SUBLANES:SUBLANES + tc, sl] = xr_ref[:, sl].astype(F32)
    kw = cw_ref.shape[0]
    nlam = -lam_ref[:, sl]
    softplus = jnp.maximum(nlam, 0.0) + jnp.log1p(jnp.exp(-jnp.abs(nlam)))
    decay = -LRU_C * softplus
    row = lax.broadcasted_iota(jnp.int32, (SUBLANES, bw), 0)
    h = h_ref[:, sl]
    pr = tc // LRU_ROW_SPLIT
    for q in range(LRU_ROW_SPLIT):
        base = SUBLANES + q * pr
        u = cb_ref[:, sl] + cw_ref[kw - 1:kw, sl] * ext_ref[base:base + pr, sl]
        for k in range(kw - 1):
            back = kw - 1 - k
            u = u + cw_ref[k:k + 1, sl] * ext_ref[base - back:base - back + pr, sl]
        ub = u.astype(BF16)
        r = _sigmoid(jnp.dot(ub, wa_ref[n], preferred_element_type=F32) + ba_ref[:, sl])
        ig = _sigmoid(jnp.dot(ub, wx_ref[n], preferred_element_type=F32) + bx_ref[:, sl])
        log_a = r * decay
        a_all = jnp.exp(log_a)
        b_all = jnp.sqrt(-jnp.tanh(log_a) * (a_all * a_all + 1.0)) * (ig * u)

        out = []
        for k in range(pr // SUBLANES):
            rows = slice(k * SUBLANES, (k + 1) * SUBLANES)
            a, b = a_all[rows, :], b_all[rows, :]
            for dist in (1, 2, 4):
                keep = row >= dist
                b = b + a * jnp.where(keep, pltpu.roll(b, dist, 0), 0.0)
                a = a * jnp.where(keep, pltpu.roll(a, dist, 0), 1.0)
            hs = a * h + b
            out.append(hs)
            h = hs[SUBLANES - 1:SUBLANES, :]
        rows = slice(q * pr, (q + 1) * pr)
        out_ref[rows, sl] = (gg_ref[rows, sl].astype(F32) * jnp.concatenate(out, axis=0)).astype(BF16)
    h_ref[:, sl] = h
    ext_ref[0:SUBLANES, sl] = ext_ref[tc:tc + SUBLANES, sl]


def _attn_kernel(sink_ref, q_ref, kvo_ref, kvp_ref, o_ref, *, layer):
    tq = q_ref.shape[0]
    first_valid_key = jnp.where(pl.program_id(1) == 0, WINDOW, 0)
    pairs = N_HEADS // N_KV_HEADS // 2
    rows = pairs * WINDOW
    lane = lax.broadcasted_iota(jnp.int32, (2 * WINDOW, LANES), 1)
    lo = lane < HEAD_DIM
    qpos = lax.broadcasted_iota(jnp.int32, (rows, 2 * WINDOW), 0) % WINDOW
    kpos = lax.broadcasted_iota(jnp.int32, (rows, 2 * WINDOW), 1)
    band = (kpos > qpos) & (kpos <= qpos + WINDOW)
    pair_of_row = lax.broadcasted_iota(jnp.int32, (rows, 1), 0) // WINDOW
    out_lo = lax.broadcasted_iota(jnp.int32, (rows, LANES), 1) < HEAD_DIM

    for jb in range(tq // WINDOW):
        own = kvo_ref[jb * WINDOW:(jb + 1) * WINDOW, :]
        prev = kvp_ref[...] if jb == 0 else kvo_ref[(jb - 1) * WINDOW:jb * WINDOW, :]
        kv = jnp.concatenate([prev, own], axis=0).astype(F32)
        kk, vv = kv[:, :LANES], kv[:, LANES:]
        kk_rot = pltpu.roll(kk, HEAD_DIM, 1)
        vv_rot = pltpu.roll(vv, HEAD_DIM, 1)
        if jb == 0:
            mask = band & (kpos >= first_valid_key)
        else:
            mask = band
        for kvh in range(N_KV_HEADS):
            if kvh == 0:
                k_top, k_bot, v_top, v_bot = kk, kk_rot, vv, vv_rot
            else:
                k_top, k_bot, v_top, v_bot = kk_rot, kk, vv_rot, vv
            kbd = jnp.concatenate([jnp.where(lo, k_top, 0.0), jnp.where(lo, 0.0, k_bot)], axis=0).astype(BF16)
            vbd = jnp.concatenate([jnp.where(lo, v_top, 0.0), jnp.where(lo, 0.0, v_bot)], axis=0).astype(BF16)
            qs = jnp.concatenate(
                [q_ref[jb * WINDOW:(jb + 1) * WINDOW, (pairs * kvh + p) * LANES:(pairs * kvh + p + 1) * LANES]
                 for p in range(pairs)], axis=0)
            qs = (qs.astype(F32) * SOFTMAX_SCALE).astype(BF16)
            s = lax.dot_general(qs, kbd, (((1,), (1,)), ((), ())), preferred_element_type=F32)
            probs, inv = [], []
            for half in range(2):
                sh = jnp.where(mask, s[:, half * 2 * WINDOW:(half + 1) * 2 * WINDOW], MASK_VALUE)
                sink = jnp.zeros((rows, 1), F32)
                for p in range(pairs):
                    head = (N_HEADS // N_KV_HEADS) * kvh + 2 * p + half
                    sink = jnp.where(pair_of_row == p, sink_ref[layer, head], sink)
                m = jnp.maximum(jnp.max(sh, axis=-1, keepdims=True), sink)
                e = jnp.exp(sh - m)
                den = jnp.sum(e, axis=-1, keepdims=True) + jnp.exp(sink - m)
                probs.append(e.astype(BF16))
                inv.append(1.0 / den)
            pv = jnp.dot(jnp.concatenate(probs, axis=1), vbd, preferred_element_type=F32)
            pv = pv * jnp.where(out_lo, inv[0], inv[1])
            for p in range(pairs):
                col = (pairs * kvh + p) * LANES
                o_ref[jb * WINDOW:(jb + 1) * WINDOW, col:col + LANES] = (
                    pv[p * WINDOW:(p + 1) * WINDOW, :].astype(o_ref.dtype))


def _attention(z, sinks, layer, batch, seq, q_block, kv_block):
    tok = z.shape[0]
    width = N_HEADS * HEAD_DIM
    kvw = 2 * N_KV_HEADS * HEAD_DIM
    nt = seq // ATTN_TQ
    per = ATTN_TQ // WINDOW
    kern = functools.partial(_attn_kernel, layer=layer)
    return pl.pallas_call(
        kern,
        grid=(batch, nt),
        in_specs=[
            pl.BlockSpec(memory_space=pltpu.SMEM),
            pl.BlockSpec((ATTN_TQ, width), lambda b, i: (b * nt + i, q_block)),
            pl.BlockSpec((ATTN_TQ, kvw), lambda b, i: (b * nt + i, kv_block)),
            pl.BlockSpec((WINDOW, kvw), lambda b, i: (jnp.maximum((b * nt + i) * per - 1, 0), kv_block)),
        ],
        out_specs=pl.BlockSpec((ATTN_TQ, width), lambda b, i: (b * nt + i, 0)),
        out_shape=jax.ShapeDtypeStruct((tok, width), BF16),
        compiler_params=_params("arbitrary", "arbitrary"),
        name="swattn",
    )(sinks, z, z, z)


def _mixer_kernel(xr_ref, gg_ref, cw_ref, cb_ref, wla_ref, ba_ref, wlx_ref, bx_ref, lam_ref,
                  x_ref, attn_ref, sga_ref, sgb_ref, g2_ref, wr_ref, wa_ref, wo_ref, o_ref,
                  rnn_ref, rnp_ref, mrg_ref, mrp_ref, ext_ref, h_ref, *, n_tiles, tiles_per_seq):
    j = pl.program_id(0)
    d = o_ref.shape[1]
    nb, bw, _ = wla_ref.shape
    nchunk = d // MIXER_TN
    blocks_per_chunk = nb // nchunk

    @pl.when(j == 0)
    def _():
        rnn_ref[...] = jnp.zeros_like(rnn_ref)
        mrg_ref[...] = jnp.zeros_like(mrg_ref)

    @pl.when(jnp.minimum(j, n_tiles - 1) % tiles_per_seq == 0)
    def _():
        ext_ref[0:SUBLANES, :] = jnp.zeros((SUBLANES, ext_ref.shape[1]), F32)
        h_ref[...] = jnp.zeros_like(h_ref)

    rnp_ref[...] = rnn_ref[...]
    mrp_ref[...] = mrg_ref[...]

    def lru(n):
        _lru_block(n, xr_ref, gg_ref, cw_ref, cb_ref, wla_ref, ba_ref, wlx_ref, bx_ref, lam_ref, ext_ref, h_ref,
                   rnn_ref)

    for c in range(nchunk):
        cs = slice(c * MIXER_TN, (c + 1) * MIXER_TN)
        br = jnp.dot(rnp_ref[...], wr_ref[:, cs], preferred_element_type=F32)
        ba = jnp.dot(attn_ref[...], wa_ref[:, cs], preferred_element_type=F32)
        mrg_ref[:, cs] = (sga_ref[:, cs].astype(F32) * br + sgb_ref[:, cs].astype(F32) * ba).astype(BF16)
        for n in range(c * blocks_per_chunk, c * blocks_per_chunk + blocks_per_chunk // 2):
            lru(n)
        proj = jnp.dot(mrp_ref[...], wo_ref[:, cs], preferred_element_type=F32)
        o_ref[:, cs] = x_ref[:, cs] + g2_ref[:, cs] * proj
        for n in range(c * blocks_per_chunk + blocks_per_chunk // 2, (c + 1) * blocks_per_chunk):
            lru(n)


def _mixer(x, attn, z, mod_l, conv_w, conv_b, wla, ba, wlx, bx, lam, w_br_rnn, w_br_attn, w_out,
           layer, seq, xr_block, gr_block):
    tok, d = x.shape
    width = conv_w.shape[2]
    n_tiles = tok // MIXER_TM
    tps = seq // MIXER_TM
    resident = pl.Buffered(1)
    rec_tile = lambda j: jnp.minimum(j, n_tiles - 1)
    mrg_tile = lambda j: jnp.clip(j - 1, 0, n_tiles - 1)
    out_tile = lambda j: jnp.maximum(j - 2, 0)
    vec = lambda: pl.BlockSpec((None, 1, width), lambda j: (layer, 0, 0))
    blk = lambda: pl.BlockSpec((None,) + wla.shape[1:], lambda j: (layer, 0, 0, 0))
    kern = functools.partial(_mixer_kernel, n_tiles=n_tiles, tiles_per_seq=tps)
    return pl.pallas_call(
        kern,
        grid=(n_tiles + 2,),
        in_specs=[
            pl.BlockSpec((MIXER_TM, width), lambda j: (rec_tile(j), xr_block)),
            pl.BlockSpec((MIXER_TM, width), lambda j: (rec_tile(j), gr_block)),
            pl.BlockSpec((None,) + conv_w.shape[1:], lambda j: (layer, 0, 0)),
            vec(), blk(), vec(), blk(), vec(), vec(),
            pl.BlockSpec((MIXER_TM, d), lambda j: (out_tile(j), 0)),
            pl.BlockSpec((MIXER_TM, width), lambda j: (mrg_tile(j), 0)),
            pl.BlockSpec((MIXER_TM, d), lambda j: (mrg_tile(j), 0)),
            pl.BlockSpec((MIXER_TM, d), lambda j: (mrg_tile(j), 1)),
            _mod_spec(d, lambda j: out_tile(j) // tps, 5),
            pl.BlockSpec((None, width, d), lambda j: (layer, 0, 0), pipeline_mode=resident),
            pl.BlockSpec((None, width, d), lambda j: (layer, 0, 0), pipeline_mode=resident),
            pl.BlockSpec((None, d, d), lambda j: (layer, 0, 0), pipeline_mode=resident),
        ],
        out_specs=pl.BlockSpec((MIXER_TM, d), lambda j: (out_tile(j), 0)),
        out_shape=jax.ShapeDtypeStruct((tok, d), F32),
        scratch_shapes=[
            pltpu.VMEM((MIXER_TM, width), BF16),
            pltpu.VMEM((MIXER_TM, width), BF16),
            pltpu.VMEM((MIXER_TM, d), BF16),
            pltpu.VMEM((MIXER_TM, d), BF16),
            pltpu.VMEM((MIXER_TM + SUBLANES, width), F32),
            pltpu.VMEM((1, width), F32),
        ],
        compiler_params=_params("arbitrary"),
        name="mixer",
    )(z, z, conv_w, conv_b, wla, ba, wlx, bx, lam, x, attn, z, z, mod_l, w_br_rnn, w_br_attn, w_out)


def kernel(x, c, g_ffn1, w_ffn1_up, w_ffn1_down, g_mix, w_in, conv_w, conv_b, lru_wa, lru_ba, lru_wx,
           lru_bx, lru_lambda, attn_sinks, w_br_rnn, w_br_attn, w_out, g_ffn2, w_ffn2_up, w_ffn2_down,
           w_mod, b_mod, g_final):
    batch, seq, d = x.shape
    depth = w_mod.shape[0]
    lru_w = conv_w.shape[2]
    attn_w = N_HEADS * HEAD_DIM
    kv_w = N_KV_HEADS * HEAD_DIM

    gate_col0 = 2 * lru_w + attn_w + 2 * kv_w
    gates_w = 2 * d
    xr_block = gates_w // lru_w
    gr_block = xr_block + 1
    q_block = (gates_w + 2 * lru_w) // attn_w
    kv_block = (gates_w + 2 * lru_w + attn_w) // (2 * kv_w)
    wbr, wba, wo = w_br_rnn.astype(BF16), w_br_attn.astype(BF16), w_out.astype(BF16)
    wa_b, wx_b = lru_wa.astype(BF16), lru_wx.astype(BF16)

    vec3 = lambda a: a.reshape(depth, 1, a.shape[-1])
    g1, gm, g2 = vec3(g_ffn1), vec3(g_mix), vec3(g_ffn2)
    cb, ba, bx, lam = vec3(conv_b), vec3(lru_ba), vec3(lru_bx), vec3(lru_lambda)
    gf = g_final.reshape(1, d)

    mod_rows = 2 * SUBLANES
    c_pad = jnp.pad(c, ((0, mod_rows - batch), (0, 0)))
    mod, up_bf, dn_bf = _modulation(c_pad, w_mod, b_mod, w_ffn1_up, w_ffn1_down)

    xs = x.reshape(batch * seq, d)
    for l in range(depth):
        mod_l = mod[l].reshape(mod_rows, 1, N_MOD * d)
        xs, up_bf, dn_bf = _ffn(xs, g1, l, mod_l, 0, up_bf, dn_bf, seq, gf, False,
                                (w_ffn2_up, w_ffn2_down, l))
        h = _normmod_call(xs, gm, mod_l, 3, l, seq)
        z = _inproj(h, w_in, l, gate_col0)
        attn = _attention(z, attn_sinks, l, batch, seq, q_block, kv_block)
        xs = _mixer(xs, attn, z, mod_l, conv_w, cb, wa_b, ba, wx_b, bx, lam, wbr, wba, wo,
                    l, seq, xr_block, gr_block)
        if l + 1 < depth:
            xs, up_bf, dn_bf = _ffn(xs, g2, l, mod_l, 6, up_bf, dn_bf, seq, gf, False,
                                    (w_ffn1_up, w_ffn1_down, l + 1))
        else:
            xs, = _ffn(xs, g2, l, mod_l, 6, up_bf, dn_bf, seq, gf, True, None)
    return xs.reshape(batch, seq, d)
```

```python
import functools

import jax
import jax.numpy as jnp
from jax import lax
from jax.experimental import pallas as pl
from jax.experimental.pallas import tpu as pltpu

F32 = jnp.float32
BF16 = jnp.bfloat16

N_HEADS = 16
N_KV_HEADS = 2
HEAD_DIM = 64
WINDOW = 128
LRU_C = 8.0
FFN_RES = 0.5
EPS = 1e-6
N_MOD = 9
SOFTMAX_SCALE = HEAD_DIM ** -0.5
assert SOFTMAX_SCALE == 2.0 ** -3
MASK_VALUE = -1e30

VMEM_LIMIT_BYTES = 56 * 1024 * 1024
SUBLANES = 8
LANES = 128

MOD_TN = 1024
NORM_TM = 1024
NORM_ROWS = 32
FFN_TM = 1024
FFN_TF = 512
FFN_UP_SPLIT = 2
FFN_CAST_STEPS = 64
FFN_CAST_DOWN_ROWS = 128
FFN_NORM_PIECES = 4
FFN_XROWS = 128
FFN_DOWN_TN = 512
INPROJ_TM = 1024
INPROJ_TN = 1024
ATTN_TQ = 512
MIXER_TM = 256
MIXER_TN = 256


def _params(*sem):
    return pltpu.CompilerParams(dimension_semantics=sem, vmem_limit_bytes=VMEM_LIMIT_BYTES)


def _sigmoid(x):
    return 0.5 * jnp.tanh(0.5 * x) + 0.5


def _gelu_tanh(x):
    c = 0.7978845608028654
    return 0.5 * x * (1.0 + jnp.tanh(c * (x + 0.044715 * (x * x * x))))


def _rmsnorm(x, g):
    ms = jnp.mean(x * x, axis=-1, keepdims=True)
    return x * lax.rsqrt(ms + EPS) * g


def _normmod(x, g, shift, scale):
    return _rmsnorm(x, g) * (1.0 + scale) + shift


def _mod_kernel(c_ref, w_ref, b_ref, cui_ref, cdi_ref, o_ref, cuo_ref, cdo_ref):
    c = c_ref[...]
    ca = (c * _sigmoid(c)).astype(BF16)
    o_ref[...] = jnp.dot(ca, w_ref[...].astype(BF16), preferred_element_type=F32) + b_ref[...]

    @pl.when(pl.program_id(0) * pl.num_programs(1) + pl.program_id(1) < FFN_CAST_STEPS)
    def _():
        cuo_ref[...] = cui_ref[...].astype(BF16)
        cdo_ref[...] = cdi_ref[...].astype(BF16)


def _cast_specs(src_up, src_down, src_layer, step_of):
    _, d, two_dff = src_up.shape
    dff = src_down.shape[1]
    up_rows, down_rows = d // FFN_CAST_STEPS, FFN_CAST_DOWN_ROWS
    assert dff // down_rows <= FFN_CAST_STEPS
    up_blk = lambda *i: jnp.minimum(step_of(*i), FFN_CAST_STEPS - 1)
    down_blk = lambda *i: jnp.minimum(step_of(*i), dff // down_rows - 1)
    in_specs = [pl.BlockSpec((None, up_rows, two_dff), lambda *i: (src_layer, up_blk(*i), 0)),
                pl.BlockSpec((None, down_rows, d), lambda *i: (src_layer, down_blk(*i), 0))]
    out_specs = [pl.BlockSpec((up_rows, two_dff), lambda *i: (up_blk(*i), 0)),
                 pl.BlockSpec((down_rows, d), lambda *i: (down_blk(*i), 0))]
    out_shape = [jax.ShapeDtypeStruct((d, two_dff), BF16), jax.ShapeDtypeStruct((dff, d), BF16)]
    return in_specs, out_specs, out_shape


def _modulation(c_pad, w_mod, b_mod, w_up, w_down):
    depth, d, nd = w_mod.shape
    rows = c_pad.shape[0]
    nj = nd // MOD_TN
    assert depth * nj >= FFN_CAST_STEPS
    cast_in, cast_out, cast_shape = _cast_specs(w_up, w_down, 0, lambda l, j: l * nj + j)
    return pl.pallas_call(
        _mod_kernel,
        grid=(depth, nj),
        in_specs=[
            pl.BlockSpec((rows, d), lambda l, j: (0, 0)),
            pl.BlockSpec((None, d, MOD_TN), lambda l, j: (l, 0, j)),
            pl.BlockSpec((None, 1, MOD_TN), lambda l, j: (l, 0, j)),
        ] + cast_in,
        out_specs=[pl.BlockSpec((None, rows, MOD_TN), lambda l, j: (l, 0, j))] + cast_out,
        out_shape=[jax.ShapeDtypeStruct((depth, rows, nd), F32)] + cast_shape,
        compiler_params=_params("arbitrary", "arbitrary"),
        name="modulation",
    )(c_pad, w_mod, b_mod.reshape(depth, 1, nd), w_up, w_down)


def _mod_spec(d, batch_of, chunk):
    return pl.BlockSpec((None, 1, d), lambda *idx: (batch_of(*idx), 0, chunk))


def _normmod_kernel(x_ref, g_ref, sh_ref, sc_ref, o_ref):
    g, sh, sc = g_ref[...], sh_ref[...], sc_ref[...]

    def body(k, carry):
        rows = pl.ds(pl.multiple_of(k * NORM_ROWS, NORM_ROWS), NORM_ROWS)
        o_ref[rows, :] = _normmod(x_ref[rows, :], g, sh, sc).astype(o_ref.dtype)
        return carry

    lax.fori_loop(0, x_ref.shape[0] // NORM_ROWS, body, 0, unroll=2)


def _normmod_call(x, g, mod_l, chunk0, layer, seq):
    tok, d = x.shape
    tpb = seq // NORM_TM
    return pl.pallas_call(
        _normmod_kernel,
        grid=(tok // NORM_TM,),
        in_specs=[
            pl.BlockSpec((NORM_TM, d), lambda i: (i, 0)),
            pl.BlockSpec((None, 1, d), lambda i: (layer, 0, 0)),
            _mod_spec(d, lambda i: i // tpb, chunk0),
            _mod_spec(d, lambda i: i // tpb, chunk0 + 1),
        ],
        out_specs=pl.BlockSpec((NORM_TM, d), lambda i: (i, 0)),
        out_shape=jax.ShapeDtypeStruct((tok, d), BF16),
        compiler_params=_params("arbitrary"),
        name="normmod",
    )(x, g, mod_l, mod_l)


def _ffn_kernel(*refs, nf, n_tiles, final_norm, cast_next):
    (xa_ref, g_ref, sh_ref, sc_ref, x_ref, ga_ref, wg_ref, wu_ref, wd_ref, gf_ref), refs = refs[:10], refs[10:]
    if cast_next:
        (cui_ref, cdi_ref, o_ref, cuo_ref, cdo_ref), refs = refs[:5], refs[5:]
    else:
        o_ref, refs = refs[0], refs[1:]
    hn_ref, hc_ref, act_ref = refs
    s = pl.program_id(0)
    tm, d = o_ref.shape
    nx = tm // FFN_XROWS
    t = s - nx
    tu = jnp.clip(t, 0, n_tiles * nf - 1)
    iu, fu = tu // nf, tu % nf
    fd = jnp.maximum(t - 1, 0) % nf
    slot = s % 2

    if cast_next:
        @pl.when(s < FFN_CAST_STEPS)
        def _():
            cuo_ref[...] = cui_ref[...].astype(BF16)
            cdo_ref[...] = cdi_ref[...].astype(BF16)

    piece_rows = FFN_XROWS // FFN_NORM_PIECES
    pieces_per_half = FFN_NORM_PIECES // FFN_UP_SPLIT

    def norm_piece(p, rchunk):
        rs = slice(p * piece_rows, (p + 1) * piece_rows)
        r0 = pl.multiple_of(rchunk * FFN_XROWS + p * piece_rows, piece_rows)
        hn = _normmod(xa_ref[rs, :], g_ref[...], sh_ref[...], sc_ref[...])
        hn_ref[pl.ds(r0, piece_rows), :] = hn.astype(BF16)

    @pl.when(t < 0)
    def _():
        for p in range(FFN_NORM_PIECES):
            norm_piece(p, s)

    @pl.when(t >= 0)
    def _():
        @pl.when(t == 0)
        def _():
            act_ref[...] = jnp.zeros_like(act_ref)

        @pl.when(fd == 0)
        def _():
            o_ref[...] = jnp.zeros_like(o_ref)

        @pl.when(fu == 0)
        def _():
            hc_ref[...] = hn_ref[...]

        ahead_chunk = jnp.minimum(fu, nx - 1)

        act_prev = act_ref[1 - slot]
        coef = FFN_RES * ga_ref[...]
        for n in range(d // FFN_DOWN_TN):
            cs = slice(n * FFN_DOWN_TN, (n + 1) * FFN_DOWN_TN)
            o_ref[:, cs] += coef[:, cs] * jnp.dot(act_prev, wd_ref[:, cs], preferred_element_type=F32)

        r0 = pl.multiple_of(jnp.minimum(fd, nx - 1) * FFN_XROWS, FFN_XROWS)
        take = jnp.where(fd < nx, 1.0, 0.0)
        o_ref[pl.ds(r0, FFN_XROWS), :] += take * x_ref[...]

        for m in range(FFN_UP_SPLIT):
            rs = slice(m * tm // FFN_UP_SPLIT, (m + 1) * tm // FFN_UP_SPLIT)
            hm = hc_ref[rs, :]
            gate = jnp.dot(hm, wg_ref[...], preferred_element_type=F32)
            for p in range(m * pieces_per_half, (m + 1) * pieces_per_half):
                norm_piece(p, ahead_chunk)
            up = jnp.dot(hm, wu_ref[...], preferred_element_type=F32)
            act_ref[slot, rs, :] = (gate * _sigmoid(gate) * up).astype(BF16)

        if final_norm:
            @pl.when(fd == nf - 1)
            def _():
                o_ref[...] = _rmsnorm(o_ref[...], gf_ref[...])


def _ffn(x, g, g_layer, mod_l, chunk0, w_up, w_down, seq, g_final, final_norm, cast_src):
    tok, d = x.shape
    dff = w_down.shape[0]
    nf = dff // FFN_TF
    n_tiles = tok // FFN_TM
    n_chunks = n_tiles * nf
    n_steps = FFN_TM // FFN_XROWS + n_chunks + 1
    tpb = seq // FFN_TM
    nx = FFN_TM // FFN_XROWS
    up_chunk = lambda s: jnp.clip(s - nx, 0, n_chunks - 1)
    down_tile = lambda s: jnp.maximum(s - nx - 1, 0) // nf
    down_chunk = lambda s: jnp.maximum(s - nx - 1, 0) % nf
    ahead_tile = lambda s: jnp.where(s < nx, 0, jnp.minimum(up_chunk(s) // nf + 1, n_tiles - 1))
    ahead_rows = lambda s: jnp.where(s < nx, s, jnp.minimum(up_chunk(s) % nf, nx - 1))
    in_specs = [
        pl.BlockSpec((FFN_XROWS, d), lambda s: (ahead_tile(s) * nx + ahead_rows(s), 0)),
        pl.BlockSpec((None, 1, d), lambda s: (g_layer, 0, 0)),
        _mod_spec(d, lambda s: ahead_tile(s) // tpb, chunk0),
        _mod_spec(d, lambda s: ahead_tile(s) // tpb, chunk0 + 1),
        pl.BlockSpec((FFN_XROWS, d), lambda s: (down_tile(s) * nx + jnp.minimum(down_chunk(s), nx - 1), 0)),
        _mod_spec(d, lambda s: down_tile(s) // tpb, chunk0 + 2),
        pl.BlockSpec((d, FFN_TF), lambda s: (0, up_chunk(s) % nf)),
        pl.BlockSpec((d, FFN_TF), lambda s: (0, nf + up_chunk(s) % nf)),
        pl.BlockSpec((FFN_TF, d), lambda s: (down_chunk(s), 0)),
        pl.BlockSpec((1, d), lambda s: (0, 0)),
    ]
    args = [x, g, mod_l, mod_l, x, mod_l, w_up, w_up, w_down, g_final]
    out_specs = [pl.BlockSpec((FFN_TM, d), lambda s: (down_tile(s), 0))]
    out_shape = [jax.ShapeDtypeStruct((tok, d), F32)]
    if cast_src is not None:
        src_up, src_down, src_layer = cast_src
        assert FFN_CAST_STEPS <= n_steps
        cast_in, cast_out, cast_shape = _cast_specs(src_up, src_down, src_layer, lambda s: s)
        in_specs += cast_in
        args += [src_up, src_down]
        out_specs += cast_out
        out_shape += cast_shape
    kern = functools.partial(_ffn_kernel, nf=nf, n_tiles=n_tiles, final_norm=final_norm,
                             cast_next=cast_src is not None)
    return pl.pallas_call(
        kern,
        grid=(n_steps,),
        in_specs=in_specs,
        out_specs=out_specs,
        out_shape=out_shape,
        scratch_shapes=[pltpu.VMEM((FFN_TM, d), BF16), pltpu.VMEM((FFN_TM, d), BF16),
                        pltpu.VMEM((2, FFN_TM, FFN_TF), BF16)],
        compiler_params=_params("arbitrary"),
        name="ffn",
    )(*args)


INPROJ_SIGMOID_TILES = 4
INPROJ_GELU_TILE = 5
INPROJ_WSPLIT = 4
INPROJ_EPILOGUE_CHUNKS = 2


def _inproj_kernel(h_ref, *refs, n_full, tail):
    w_refs, (o_ref, wb_ref) = refs[:INPROJ_WSPLIT], refs[INPROJ_WSPLIT:]
    n = pl.program_id(0)
    sub = wb_ref.shape[1] // INPROJ_WSPLIT

    @pl.when(pl.program_id(1) == 0)
    def _():
        for k, w_ref in enumerate(w_refs):
            wb_ref[:, k * sub:(k + 1) * sub] = w_ref[...].astype(BF16)

    h = h_ref[...]

    def project(fn):
        cw = wb_ref.shape[1] // INPROJ_EPILOGUE_CHUNKS
        for k in range(INPROJ_EPILOGUE_CHUNKS):
            cs = slice(k * cw, (k + 1) * cw)
            o_ref[:, cs] = fn(jnp.dot(h, wb_ref[:, cs], preferred_element_type=F32)).astype(o_ref.dtype)

    @pl.when(n < INPROJ_SIGMOID_TILES)
    def _():
        project(_sigmoid)

    @pl.when(n == INPROJ_GELU_TILE)
    def _():
        project(_gelu_tanh)

    @pl.when((n >= INPROJ_SIGMOID_TILES) & (n != INPROJ_GELU_TILE) & (n < n_full))
    def _():
        project(lambda acc: acc)

    @pl.when(n == n_full)
    def _():
        o_ref[:, :tail] = jnp.dot(h, wb_ref[:, :tail], preferred_element_type=F32).astype(o_ref.dtype)


def _inproj(h, w_in, layer, gate_col0):
    tok, d = h.shape
    ncol = w_in.shape[2]
    n_full = ncol // INPROJ_TN
    tail = ncol - n_full * INPROJ_TN
    sub = INPROJ_TN // INPROJ_WSPLIT
    gate_blk0 = gate_col0 // sub
    last_blk = ncol // sub - 1

    def w_spec(k):
        def index(n, i):
            base = jnp.where(n < INPROJ_SIGMOID_TILES, gate_blk0 + INPROJ_WSPLIT * n,
                             INPROJ_WSPLIT * (n - INPROJ_SIGMOID_TILES))
            return (layer, 0, jnp.minimum(base + k, last_blk))
        return pl.BlockSpec((None, d, sub), index)

    kern = functools.partial(_inproj_kernel, n_full=n_full, tail=tail)
    return pl.pallas_call(
        kern,
        grid=(n_full + 1, tok // INPROJ_TM),
        in_specs=[pl.BlockSpec((INPROJ_TM, d), lambda n, i: (i, 0))] + [w_spec(k) for k in range(INPROJ_WSPLIT)],
        out_specs=pl.BlockSpec((INPROJ_TM, INPROJ_TN), lambda n, i: (i, n)),
        out_shape=jax.ShapeDtypeStruct((tok, ncol), BF16),
        scratch_shapes=[pltpu.VMEM((d, INPROJ_TN), BF16)],
        compiler_params=_params("arbitrary", "arbitrary"),
        name="inproj",
    )(h, *([w_in] * INPROJ_WSPLIT))


def _lru_gate_logits(n, xr_ref, cw_ref, cb_ref, wa_ref, ba_ref, wx_ref, bx_ref, ext_ref):
    tc = xr_ref.shape[0]
    bw = wa_ref.shape[1]
    sl = slice(n * bw, (n + 1) * bw)
    ext_ref[SUBLANES:SUBLANES + tc, sl] = xr_ref[:, sl].astype(F32)
    kw = cw_ref.shape[0]
    u = cb_ref[:, sl] + cw_ref[kw - 1:kw, sl] * ext_ref[SUBLANES:SUBLANES + tc, sl]
    for k in range(kw - 1):
        back = kw - 1 - k
        u = u + cw_ref[k:k + 1, sl] * ext_ref[SUBLANES - back:SUBLANES - back + tc, sl]
    ext_ref[0:SUBLANES, sl] = ext_ref[tc:tc + SUBLANES, sl]
    ub = u.astype(BF16)
    r_logit = jnp.dot(ub, wa_ref[n], preferred_element_type=F32) + ba_ref[:, sl]
    i_logit = jnp.dot(ub, wx_ref[n], preferred_element_type=F32) + bx_ref[:, sl]
    return u, r_logit, i_logit


def _lru_scan(n, gates, gg_ref, lam_ref, h_ref, out_ref):
    u, r_logit, i_logit = gates
    tc, bw = u.shape
    sl = slice(n * bw, (n + 1) * bw)
    nlam = -lam_ref[:, sl]
    softplus = jnp.maximum(nlam, 0.0) + jnp.log1p(jnp.exp(-jnp.abs(nlam)))
    log_a = _sigmoid(r_logit) * (-LRU_C * softplus)
    a_all = jnp.exp(log_a)
    b_all = jnp.sqrt(-jnp.tanh(log_a) * (a_all * a_all + 1.0)) * (_sigmoid(i_logit) * u)

    row = lax.broadcasted_iota(jnp.int32, (SUBLANES, bw), 0)
    h = h_ref[:, sl]
    out = []
    for k in range(tc // SUBLANES):
        rows = slice(k * SUBLANES, (k + 1) * SUBLANES)
        a, b = a_all[rows, :], b_all[rows, :]
        for dist in (1, 2, 4):
            keep = row >= dist
            b = b + a * jnp.where(keep, pltpu.roll(b, dist, 0), 0.0)
            a = a * jnp.where(keep, pltpu.roll(a, dist, 0), 1.0)
        hs = a * h + b
        out.append(hs)
        h = hs[SUBLANES - 1:SUBLANES, :]
    h_ref[:, sl] = h
    out_ref[:, sl] = (gg_ref[:, sl].astype(F32) * jnp.concatenate(out, axis=0)).astype(BF16)


def _attn_kernel(sink_ref, q_ref, kvo_ref, kvp_ref, o_ref, *, layer):
    tq = q_ref.shape[0]
    first_valid_key = jnp.where(pl.program_id(1) == 0, WINDOW, 0)
    pairs = N_HEADS // N_KV_HEADS // 2
    rows = pairs * WINDOW
    lane = lax.broadcasted_iota(jnp.int32, (2 * WINDOW, LANES), 1)
    lo = lane < HEAD_DIM
    qpos = lax.broadcasted_iota(jnp.int32, (rows, 2 * WINDOW), 0) % WINDOW
    kpos = lax.broadcasted_iota(jnp.int32, (rows, 2 * WINDOW), 1)
    band = (kpos > qpos) & (kpos <= qpos + WINDOW)
    bias_rest = jnp.where(band, 0.0, MASK_VALUE)
    bias_first = jnp.where(band & (kpos >= first_valid_key), 0.0, MASK_VALUE)
    pair_of_row = lax.broadcasted_iota(jnp.int32, (rows, 1), 0) // WINDOW
    out_lo = lax.broadcasted_iota(jnp.int32, (rows, LANES), 1) < HEAD_DIM

    def sink_col(kvh, half):
        col = jnp.zeros((rows, 1), F32)
        for p in range(pairs):
            head = (N_HEADS // N_KV_HEADS) * kvh + 2 * p + half
            col = jnp.where(pair_of_row == p, sink_ref[layer, head], col)
        return col

    sink_cols = [[sink_col(kvh, half) for half in range(2)] for kvh in range(N_KV_HEADS)]

    for jb in range(tq // WINDOW):
        own = kvo_ref[jb * WINDOW:(jb + 1) * WINDOW, :]
        prev = kvp_ref[...] if jb == 0 else kvo_ref[(jb - 1) * WINDOW:jb * WINDOW, :]
        kv = jnp.concatenate([prev, own], axis=0).astype(F32)
        kk, vv = kv[:, :LANES], kv[:, LANES:]
        kk_rot = pltpu.roll(kk, HEAD_DIM, 1)
        vv_rot = pltpu.roll(vv, HEAD_DIM, 1)
        bias = bias_first if jb == 0 else bias_rest
        for kvh in range(N_KV_HEADS):
            if kvh == 0:
                k_top, k_bot, v_top, v_bot = kk, kk_rot, vv, vv_rot
            else:
                k_top, k_bot, v_top, v_bot = kk_rot, kk, vv_rot, vv
            kbd = jnp.concatenate([jnp.where(lo, k_top, 0.0), jnp.where(lo, 0.0, k_bot)], axis=0).astype(BF16)
            vbd = jnp.concatenate([jnp.where(lo, v_top, 0.0), jnp.where(lo, 0.0, v_bot)], axis=0).astype(BF16)
            qs = jnp.concatenate(
                [q_ref[jb * WINDOW:(jb + 1) * WINDOW, (pairs * kvh + p) * LANES:(pairs * kvh + p + 1) * LANES]
                 for p in range(pairs)], axis=0)
            qs = (qs.astype(F32) * SOFTMAX_SCALE).astype(BF16)
            s = lax.dot_general(qs, kbd, (((1,), (1,)), ((), ())), preferred_element_type=F32)
            probs, inv = [], []
            for half in range(2):
                sh = s[:, half * 2 * WINDOW:(half + 1) * 2 * WINDOW] + bias
                sink = sink_cols[kvh][half]
                m = jnp.maximum(jnp.max(sh, axis=-1, keepdims=True), sink)
                e = jnp.exp(sh - m)
                den = jnp.sum(e, axis=-1, keepdims=True) + jnp.exp(sink - m)
                probs.append(e.astype(BF16))
                inv.append(1.0 / den)
            pv = jnp.dot(jnp.concatenate(probs, axis=1), vbd, preferred_element_type=F32)
            pv = pv * jnp.where(out_lo, inv[0], inv[1])
            for p in range(pairs):
                col = (pairs * kvh + p) * LANES
                o_ref[jb * WINDOW:(jb + 1) * WINDOW, col:col + LANES] = (
                    pv[p * WINDOW:(p + 1) * WINDOW, :].astype(o_ref.dtype))


def _attention(z, sinks, layer, batch, seq, q_block, kv_block):
    tok = z.shape[0]
    width = N_HEADS * HEAD_DIM
    kvw = 2 * N_KV_HEADS * HEAD_DIM
    nt = seq // ATTN_TQ
    per = ATTN_TQ // WINDOW
    kern = functools.partial(_attn_kernel, layer=layer)
    return pl.pallas_call(
        kern,
        grid=(batch, nt),
        in_specs=[
            pl.BlockSpec(memory_space=pltpu.SMEM),
            pl.BlockSpec((ATTN_TQ, width), lambda b, i: (b * nt + i, q_block)),
            pl.BlockSpec((ATTN_TQ, kvw), lambda b, i: (b * nt + i, kv_block)),
            pl.BlockSpec((WINDOW, kvw), lambda b, i: (jnp.maximum((b * nt + i) * per - 1, 0), kv_block)),
        ],
        out_specs=pl.BlockSpec((ATTN_TQ, width), lambda b, i: (b * nt + i, 0)),
        out_shape=jax.ShapeDtypeStruct((tok, width), BF16),
        compiler_params=_params("arbitrary", "arbitrary"),
        name="swattn",
    )(sinks, z, z, z)


def _mixer_kernel(xr_ref, gg_ref, cw_ref, cb_ref, wla_ref, ba_ref, wlx_ref, bx_ref, lam_ref,
                  x_ref, attn_ref, sga_ref, sgb_ref, g2_ref, wr_ref, wa_ref, wo_ref, o_ref,
                  rnn_ref, rnp_ref, mrg_ref, mrp_ref, ext_ref, h_ref, *, n_tiles, tiles_per_seq):
    j = pl.program_id(0)
    d = o_ref.shape[1]
    nb, bw, _ = wla_ref.shape
    assert d // MIXER_TN == nb

    @pl.when(j == 0)
    def _():
        rnn_ref[...] = jnp.zeros_like(rnn_ref)
        mrg_ref[...] = jnp.zeros_like(mrg_ref)

    @pl.when(jnp.minimum(j, n_tiles - 1) % tiles_per_seq == 0)
    def _():
        ext_ref[0:SUBLANES, :] = jnp.zeros((SUBLANES, ext_ref.shape[1]), F32)
        h_ref[...] = jnp.zeros_like(h_ref)

    rnp_ref[...] = rnn_ref[...]
    mrp_ref[...] = mrg_ref[...]

    def gate_logits(n):
        return _lru_gate_logits(n, xr_ref, cw_ref, cb_ref, wla_ref, ba_ref, wlx_ref, bx_ref, ext_ref)

    for n in range(nb):
        cs = slice(n * MIXER_TN, (n + 1) * MIXER_TN)
        br = jnp.dot(rnp_ref[...], wr_ref[:, cs], preferred_element_type=F32)
        ba = jnp.dot(attn_ref[...], wa_ref[:, cs], preferred_element_type=F32)
        mrg_ref[:, cs] = (sga_ref[:, cs].astype(F32) * br + sgb_ref[:, cs].astype(F32) * ba).astype(BF16)
        proj = jnp.dot(mrp_ref[...], wo_ref[:, cs], preferred_element_type=F32)
        o_ref[:, cs] = x_ref[:, cs] + g2_ref[:, cs] * proj
        _lru_scan(n, gate_logits(n), gg_ref, lam_ref, h_ref, rnn_ref)


def _mixer(x, attn, z, mod_l, conv_w, conv_b, wla, ba, wlx, bx, lam, w_br_rnn, w_br_attn, w_out,
           layer, seq, xr_block, gr_block):
    tok, d = x.shape
    width = conv_w.shape[2]
    n_tiles = tok // MIXER_TM
    tps = seq // MIXER_TM
    resident = pl.Buffered(1)
    rec_tile = lambda j: jnp.minimum(j, n_tiles - 1)
    mrg_tile = lambda j: jnp.clip(j - 1, 0, n_tiles - 1)
    out_tile = lambda j: jnp.maximum(j - 2, 0)
    vec = lambda: pl.BlockSpec((None, 1, width), lambda j: (layer, 0, 0))
    blk = lambda: pl.BlockSpec((None,) + wla.shape[1:], lambda j: (layer, 0, 0, 0))
    kern = functools.partial(_mixer_kernel, n_tiles=n_tiles, tiles_per_seq=tps)
    return pl.pallas_call(
        kern,
        grid=(n_tiles + 2,),
        in_specs=[
            pl.BlockSpec((MIXER_TM, width), lambda j: (rec_tile(j), xr_block)),
            pl.BlockSpec((MIXER_TM, width), lambda j: (rec_tile(j), gr_block)),
            pl.BlockSpec((None,) + conv_w.shape[1:], lambda j: (layer, 0, 0)),
            vec(), blk(), vec(), blk(), vec(), vec(),
            pl.BlockSpec((MIXER_TM, d), lambda j: (out_tile(j), 0)),
            pl.BlockSpec((MIXER_TM, width), lambda j: (mrg_tile(j), 0)),
            pl.BlockSpec((MIXER_TM, d), lambda j: (mrg_tile(j), 0)),
            pl.BlockSpec((MIXER_TM, d), lambda j: (mrg_tile(j), 1)),
            _mod_spec(d, lambda j: out_tile(j) // tps, 5),
            pl.BlockSpec((None, width, d), lambda j: (layer, 0, 0), pipeline_mode=resident),
            pl.BlockSpec((None, width, d), lambda j: (layer, 0, 0), pipeline_mode=resident),
            pl.BlockSpec((None, d, d), lambda j: (layer, 0, 0), pipeline_mode=resident),
        ],
        out_specs=pl.BlockSpec((MIXER_TM, d), lambda j: (out_tile(j), 0)),
        out_shape=jax.ShapeDtypeStruct((tok, d), F32),
        scratch_shapes=[
            pltpu.VMEM((MIXER_TM, width), BF16),
            pltpu.VMEM((MIXER_TM, width), BF16),
            pltpu.VMEM((MIXER_TM, d), BF16),
            pltpu.VMEM((MIXER_TM, d), BF16),
            pltpu.VMEM((MIXER_TM + SUBLANES, width), F32),
            pltpu.VMEM((1, width), F32),
        ],
        compiler_params=_params("arbitrary"),
        name="mixer",
    )(z, z, conv_w, conv_b, wla, ba, wlx, bx, lam, x, attn, z, z, mod_l, w_br_rnn, w_br_attn, w_out)


def kernel(x, c, g_ffn1, w_ffn1_up, w_ffn1_down, g_mix, w_in, conv_w, conv_b, lru_wa, lru_ba, lru_wx,
           lru_bx, lru_lambda, attn_sinks, w_br_rnn, w_br_attn, w_out, g_ffn2, w_ffn2_up, w_ffn2_down,
           w_mod, b_mod, g_final):
    batch, seq, d = x.shape
    depth = w_mod.shape[0]
    lru_w = conv_w.shape[2]
    attn_w = N_HEADS * HEAD_DIM
    kv_w = N_KV_HEADS * HEAD_DIM

    gate_col0 = 2 * lru_w + attn_w + 2 * kv_w
    gates_w = 2 * d
    xr_block = gates_w // lru_w
    gr_block = xr_block + 1
    q_block = (gates_w + 2 * lru_w) // attn_w
    kv_block = (gates_w + 2 * lru_w + attn_w) // (2 * kv_w)
    wbr, wba, wo = w_br_rnn.astype(BF16), w_br_attn.astype(BF16), w_out.astype(BF16)
    wa_b, wx_b = lru_wa.astype(BF16), lru_wx.astype(BF16)

    vec3 = lambda a: a.reshape(depth, 1, a.shape[-1])
    g1, gm, g2 = vec3(g_ffn1), vec3(g_mix), vec3(g_ffn2)
    cb, ba, bx, lam = vec3(conv_b), vec3(lru_ba), vec3(lru_bx), vec3(lru_lambda)
    gf = g_final.reshape(1, d)

    mod_rows = 2 * SUBLANES
    c_pad = jnp.pad(c, ((0, mod_rows - batch), (0, 0)))
    mod, up_bf, dn_bf = _modulation(c_pad, w_mod, b_mod, w_ffn1_up, w_ffn1_down)

    xs = x.reshape(batch * seq, d)
    for l in range(depth):
        mod_l = mod[l].reshape(mod_rows, 1, N_MOD * d)
        xs, up_bf, dn_bf = _ffn(xs, g1, l, mod_l, 0, up_bf, dn_bf, seq, gf, False,
                                (w_ffn2_up, w_ffn2_down, l))
        h = _normmod_call(xs, gm, mod_l, 3, l, seq)
        z = _inproj(h, w_in, l, gate_col0)
        attn = _attention(z, attn_sinks, l, batch, seq, q_block, kv_block)
        xs = _mixer(xs, attn, z, mod_l, conv_w, cb, wa_b, ba, wx_b, bx, lam, wbr, wba, wo,
                    l, seq, xr_block, gr_block)
        if l + 1 < depth:
            xs, up_bf, dn_bf = _ffn(xs, g2, l, mod_l, 6, up_bf, dn_bf, seq, gf, False,
                                    (w_ffn1_up, w_ffn1_down, l + 1))
        else:
            xs, = _ffn(xs, g2, l, mod_l, 6, up_bf, dn_bf, seq, gf, True, None)
    return xs.reshape(batch, seq, d)
```

```python
import functools

import jax
import jax.numpy as jnp
from jax import lax
from jax.experimental import pallas as pl
from jax.experimental.pallas import tpu as pltpu

F32 = jnp.float32
BF16 = jnp.bfloat16

N_HEADS = 16
N_KV_HEADS = 2
HEAD_DIM = 64
WINDOW = 128
LRU_C = 8.0
FFN_RES = 0.5
EPS = 1e-6
N_MOD = 9
SOFTMAX_SCALE = HEAD_DIM ** -0.5
assert SOFTMAX_SCALE == 2.0 ** -3
MASK_VALUE = -1e30

VMEM_LIMIT_BYTES = 56 * 1024 * 1024
SUBLANES = 8
LANES = 128

MOD_TN = 1024
NORM_TM = 1024
NORM_ROWS = 32
FFN_TM = 1024
FFN_TF = 512
FFN_UP_SPLIT = 2
FFN_CAST_STEPS = 64
FFN_CAST_DOWN_ROWS = 128
FFN_NORM_PIECES = 4
FFN_XROWS = 128
FFN_DOWN_TN = 512
INPROJ_TM = 1024
INPROJ_TN = 1024
ATTN_TQ = 512
MIXER_TM = 256
MIXER_TN = 256


def _params(*sem):
    return pltpu.CompilerParams(dimension_semantics=sem, vmem_limit_bytes=VMEM_LIMIT_BYTES)


def _sigmoid(x):
    return 0.5 * jnp.tanh(0.5 * x) + 0.5


def _gelu_tanh(x):
    c = 0.7978845608028654
    return 0.5 * x * (1.0 + jnp.tanh(c * (x + 0.044715 * (x * x * x))))


def _rmsnorm(x, g):
    ms = jnp.mean(x * x, axis=-1, keepdims=True)
    return x * lax.rsqrt(ms + EPS) * g


def _normmod(x, g, shift, scale):
    return _rmsnorm(x, g) * (1.0 + scale) + shift


def _mod_kernel(c_ref, w_ref, b_ref, cui_ref, cdi_ref, o_ref, cuo_ref, cdo_ref):
    c = c_ref[...]
    ca = (c * _sigmoid(c)).astype(BF16)
    o_ref[...] = jnp.dot(ca, w_ref[...].astype(BF16), preferred_element_type=F32) + b_ref[...]

    @pl.when(pl.program_id(0) * pl.num_programs(1) + pl.program_id(1) < FFN_CAST_STEPS)
    def _():
        cuo_ref[...] = cui_ref[...].astype(BF16)
        cdo_ref[...] = cdi_ref[...].astype(BF16)


def _cast_specs(src_up, src_down, src_layer, step_of):
    _, d, two_dff = src_up.shape
    dff = src_down.shape[1]
    up_rows, down_rows = d // FFN_CAST_STEPS, FFN_CAST_DOWN_ROWS
    assert dff // down_rows <= FFN_CAST_STEPS
    up_blk = lambda *i: jnp.minimum(step_of(*i), FFN_CAST_STEPS - 1)
    down_blk = lambda *i: jnp.minimum(step_of(*i), dff // down_rows - 1)
    in_specs = [pl.BlockSpec((None, up_rows, two_dff), lambda *i: (src_layer, up_blk(*i), 0)),
                pl.BlockSpec((None, down_rows, d), lambda *i: (src_layer, down_blk(*i), 0))]
    out_specs = [pl.BlockSpec((up_rows, two_dff), lambda *i: (up_blk(*i), 0)),
                 pl.BlockSpec((down_rows, d), lambda *i: (down_blk(*i), 0))]
    out_shape = [jax.ShapeDtypeStruct((d, two_dff), BF16), jax.ShapeDtypeStruct((dff, d), BF16)]
    return in_specs, out_specs, out_shape


def _modulation(c_pad, w_mod, b_mod, w_up, w_down):
    depth, d, nd = w_mod.shape
    rows = c_pad.shape[0]
    nj = nd // MOD_TN
    assert depth * nj >= FFN_CAST_STEPS
    cast_in, cast_out, cast_shape = _cast_specs(w_up, w_down, 0, lambda l, j: l * nj + j)
    return pl.pallas_call(
        _mod_kernel,
        grid=(depth, nj),
        in_specs=[
            pl.BlockSpec((rows, d), lambda l, j: (0, 0)),
            pl.BlockSpec((None, d, MOD_TN), lambda l, j: (l, 0, j)),
            pl.BlockSpec((None, 1, MOD_TN), lambda l, j: (l, 0, j)),
        ] + cast_in,
        out_specs=[pl.BlockSpec((None, rows, MOD_TN), lambda l, j: (l, 0, j))] + cast_out,
        out_shape=[jax.ShapeDtypeStruct((depth, rows, nd), F32)] + cast_shape,
        compiler_params=_params("arbitrary", "arbitrary"),
        name="modulation",
    )(c_pad, w_mod, b_mod.reshape(depth, 1, nd), w_up, w_down)


def _mod_spec(d, batch_of, chunk):
    return pl.BlockSpec((None, 1, d), lambda *idx: (batch_of(*idx), 0, chunk))


def _normmod_kernel(x_ref, g_ref, sh_ref, sc_ref, o_ref):
    g, sh, sc = g_ref[...], sh_ref[...], sc_ref[...]

    def body(k, carry):
        rows = pl.ds(pl.multiple_of(k * NORM_ROWS, NORM_ROWS), NORM_ROWS)
        o_ref[rows, :] = _normmod(x_ref[rows, :], g, sh, sc).astype(o_ref.dtype)
        return carry

    lax.fori_loop(0, x_ref.shape[0] // NORM_ROWS, body, 0, unroll=4)


def _normmod_call(x, g, mod_l, chunk0, layer, seq):
    tok, d = x.shape
    tpb = seq // NORM_TM
    return pl.pallas_call(
        _normmod_kernel,
        grid=(tok // NORM_TM,),
        in_specs=[
            pl.BlockSpec((NORM_TM, d), lambda i: (i, 0)),
            pl.BlockSpec((None, 1, d), lambda i: (layer, 0, 0)),
            _mod_spec(d, lambda i: i // tpb, chunk0),
            _mod_spec(d, lambda i: i // tpb, chunk0 + 1),
        ],
        out_specs=pl.BlockSpec((NORM_TM, d), lambda i: (i, 0)),
        out_shape=jax.ShapeDtypeStruct((tok, d), BF16),
        compiler_params=_params("arbitrary"),
        name="normmod",
    )(x, g, mod_l, mod_l)


def _ffn_kernel(*refs, nf, n_tiles, final_norm, cast_next):
    (xa_ref, g_ref, sh_ref, sc_ref, x_ref, ga_ref, wg_ref, wu_ref, wd_ref, gf_ref), refs = refs[:10], refs[10:]
    if cast_next:
        (cui_ref, cdi_ref, o_ref, cuo_ref, cdo_ref), refs = refs[:5], refs[5:]
    else:
        o_ref, refs = refs[0], refs[1:]
    hn_ref, hc_ref, act_ref = refs
    s = pl.program_id(0)
    tm, d = o_ref.shape
    nx = tm // FFN_XROWS
    t = s - nx
    tu = jnp.clip(t, 0, n_tiles * nf - 1)
    iu, fu = tu // nf, tu % nf
    fd = jnp.maximum(t - 1, 0) % nf
    slot = s % 2

    if cast_next:
        @pl.when(s < FFN_CAST_STEPS)
        def _():
            cuo_ref[...] = cui_ref[...].astype(BF16)
            cdo_ref[...] = cdi_ref[...].astype(BF16)

    piece_rows = FFN_XROWS // FFN_NORM_PIECES
    pieces_per_half = FFN_NORM_PIECES // FFN_UP_SPLIT

    def norm_piece(p, rchunk):
        rs = slice(p * piece_rows, (p + 1) * piece_rows)
        r0 = pl.multiple_of(rchunk * FFN_XROWS + p * piece_rows, piece_rows)
        hn = _normmod(xa_ref[rs, :], g_ref[...], sh_ref[...], sc_ref[...])
        hn_ref[pl.ds(r0, piece_rows), :] = hn.astype(BF16)

    @pl.when(t < 0)
    def _():
        for p in range(FFN_NORM_PIECES):
            norm_piece(p, s)

    @pl.when(t >= 0)
    def _():
        @pl.when(t == 0)
        def _():
            act_ref[...] = jnp.zeros_like(act_ref)

        @pl.when(fd == 0)
        def _():
            o_ref[...] = jnp.zeros_like(o_ref)

        @pl.when(fu == 0)
        def _():
            hc_ref[...] = hn_ref[...]

        ahead_chunk = jnp.minimum(fu, nx - 1)

        act_prev = act_ref[1 - slot]
        coef = FFN_RES * ga_ref[...]
        for n in range(d // FFN_DOWN_TN):
            cs = slice(n * FFN_DOWN_TN, (n + 1) * FFN_DOWN_TN)
            o_ref[:, cs] += coef[:, cs] * jnp.dot(act_prev, wd_ref[:, cs], preferred_element_type=F32)

        r0 = pl.multiple_of(jnp.minimum(fd, nx - 1) * FFN_XROWS, FFN_XROWS)
        take = jnp.where(fd < nx, 1.0, 0.0)
        o_ref[pl.ds(r0, FFN_XROWS), :] += take * x_ref[...]

        for m in range(FFN_UP_SPLIT):
            rs = slice(m * tm // FFN_UP_SPLIT, (m + 1) * tm // FFN_UP_SPLIT)
            hm = hc_ref[rs, :]
            gate = jnp.dot(hm, wg_ref[...], preferred_element_type=F32)
            for p in range(m * pieces_per_half, (m + 1) * pieces_per_half):
                norm_piece(p, ahead_chunk)
            up = jnp.dot(hm, wu_ref[...], preferred_element_type=F32)
            act_ref[slot, rs, :] = (gate * _sigmoid(gate) * up).astype(BF16)

        if final_norm:
            @pl.when(fd == nf - 1)
            def _():
                o_ref[...] = _rmsnorm(o_ref[...], gf_ref[...])


def _ffn(x, g, g_layer, mod_l, chunk0, w_up, w_down, seq, g_final, final_norm, cast_src):
    tok, d = x.shape
    dff = w_down.shape[0]
    nf = dff // FFN_TF
    n_tiles = tok // FFN_TM
    n_chunks = n_tiles * nf
    n_steps = FFN_TM // FFN_XROWS + n_chunks + 1
    tpb = seq // FFN_TM
    nx = FFN_TM // FFN_XROWS
    up_chunk = lambda s: jnp.clip(s - nx, 0, n_chunks - 1)
    down_tile = lambda s: jnp.maximum(s - nx - 1, 0) // nf
    down_chunk = lambda s: jnp.maximum(s - nx - 1, 0) % nf
    ahead_tile = lambda s: jnp.where(s < nx, 0, jnp.minimum(up_chunk(s) // nf + 1, n_tiles - 1))
    ahead_rows = lambda s: jnp.where(s < nx, s, jnp.minimum(up_chunk(s) % nf, nx - 1))
    in_specs = [
        pl.BlockSpec((FFN_XROWS, d), lambda s: (ahead_tile(s) * nx + ahead_rows(s), 0)),
        pl.BlockSpec((None, 1, d), lambda s: (g_layer, 0, 0)),
        _mod_spec(d, lambda s: ahead_tile(s) // tpb, chunk0),
        _mod_spec(d, lambda s: ahead_tile(s) // tpb, chunk0 + 1),
        pl.BlockSpec((FFN_XROWS, d), lambda s: (down_tile(s) * nx + jnp.minimum(down_chunk(s), nx - 1), 0)),
        _mod_spec(d, lambda s: down_tile(s) // tpb, chunk0 + 2),
        pl.BlockSpec((d, FFN_TF), lambda s: (0, up_chunk(s) % nf)),
        pl.BlockSpec((d, FFN_TF), lambda s: (0, nf + up_chunk(s) % nf)),
        pl.BlockSpec((FFN_TF, d), lambda s: (down_chunk(s), 0)),
        pl.BlockSpec((1, d), lambda s: (0, 0)),
    ]
    args = [x, g, mod_l, mod_l, x, mod_l, w_up, w_up, w_down, g_final]
    out_specs = [pl.BlockSpec((FFN_TM, d), lambda s: (down_tile(s), 0))]
    out_shape = [jax.ShapeDtypeStruct((tok, d), F32)]
    if cast_src is not None:
        src_up, src_down, src_layer = cast_src
        assert FFN_CAST_STEPS <= n_steps
        cast_in, cast_out, cast_shape = _cast_specs(src_up, src_down, src_layer, lambda s: s)
        in_specs += cast_in
        args += [src_up, src_down]
        out_specs += cast_out
        out_shape += cast_shape
    kern = functools.partial(_ffn_kernel, nf=nf, n_tiles=n_tiles, final_norm=final_norm,
                             cast_next=cast_src is not None)
    return pl.pallas_call(
        kern,
        grid=(n_steps,),
        in_specs=in_specs,
        out_specs=out_specs,
        out_shape=out_shape,
        scratch_shapes=[pltpu.VMEM((FFN_TM, d), BF16), pltpu.VMEM((FFN_TM, d), BF16),
                        pltpu.VMEM((2, FFN_TM, FFN_TF), BF16)],
        compiler_params=_params("arbitrary"),
        name="ffn",
    )(*args)


INPROJ_SIGMOID_TILES = 4
INPROJ_GELU_TILE = 5
INPROJ_WSPLIT = 4
INPROJ_EPILOGUE_CHUNKS = 2


def _inproj_kernel(h_ref, *refs, n_full, tail):
    w_refs, refs = refs[:INPROJ_WSPLIT], refs[INPROJ_WSPLIT:]
    cast_in, o_ref, cast_out, wb_ref = refs[:3], refs[3], refs[4:7], refs[7]
    n = pl.program_id(0)
    sub = wb_ref.shape[1] // INPROJ_WSPLIT

    for src_ref, dst_ref in zip(cast_in, cast_out):
        dst_ref[...] = src_ref[...].astype(BF16)

    @pl.when(pl.program_id(1) == 0)
    def _():
        for k, w_ref in enumerate(w_refs):
            wb_ref[:, k * sub:(k + 1) * sub] = w_ref[...].astype(BF16)

    h = h_ref[...]

    def project(fn):
        cw = wb_ref.shape[1] // INPROJ_EPILOGUE_CHUNKS
        for k in range(INPROJ_EPILOGUE_CHUNKS):
            cs = slice(k * cw, (k + 1) * cw)
            o_ref[:, cs] = fn(jnp.dot(h, wb_ref[:, cs], preferred_element_type=F32)).astype(o_ref.dtype)

    @pl.when(n < INPROJ_SIGMOID_TILES)
    def _():
        project(_sigmoid)

    @pl.when(n == INPROJ_GELU_TILE)
    def _():
        project(_gelu_tanh)

    @pl.when((n >= INPROJ_SIGMOID_TILES) & (n != INPROJ_GELU_TILE) & (n < n_full))
    def _():
        project(lambda acc: acc)

    @pl.when(n == n_full)
    def _():
        o_ref[:, :tail] = jnp.dot(h, wb_ref[:, :tail], preferred_element_type=F32).astype(o_ref.dtype)


def _inproj(h, w_in, layer, gate_col0, mixer_weights):
    tok, d = h.shape
    ncol = w_in.shape[2]
    n_full = ncol // INPROJ_TN
    tail = ncol - n_full * INPROJ_TN
    sub = INPROJ_TN // INPROJ_WSPLIT
    gate_blk0 = gate_col0 // sub
    last_blk = ncol // sub - 1

    def w_spec(k):
        def index(n, i):
            base = jnp.where(n < INPROJ_SIGMOID_TILES, gate_blk0 + INPROJ_WSPLIT * n,
                             INPROJ_WSPLIT * (n - INPROJ_SIGMOID_TILES))
            return (layer, 0, jnp.minimum(base + k, last_blk))
        return pl.BlockSpec((None, d, sub), index)

    n_m = tok // INPROJ_TM
    n_steps = (n_full + 1) * n_m
    cast_in, cast_out, cast_shape = [], [], []
    for w in mixer_weights:
        rows = w.shape[1] // n_steps
        assert rows * n_steps == w.shape[1] and rows % (2 * SUBLANES) == 0
        cast_in.append(pl.BlockSpec((None, rows) + w.shape[2:], lambda n, i: (layer, n * n_m + i, 0)))
        cast_out.append(pl.BlockSpec((rows,) + w.shape[2:], lambda n, i: (n * n_m + i, 0)))
        cast_shape.append(jax.ShapeDtypeStruct(w.shape[1:], BF16))

    kern = functools.partial(_inproj_kernel, n_full=n_full, tail=tail)
    return pl.pallas_call(
        kern,
        grid=(n_full + 1, n_m),
        in_specs=([pl.BlockSpec((INPROJ_TM, d), lambda n, i: (i, 0))] + [w_spec(k) for k in range(INPROJ_WSPLIT)]
                  + cast_in),
        out_specs=[pl.BlockSpec((INPROJ_TM, INPROJ_TN), lambda n, i: (i, n))] + cast_out,
        out_shape=[jax.ShapeDtypeStruct((tok, ncol), BF16)] + cast_shape,
        scratch_shapes=[pltpu.VMEM((d, INPROJ_TN), BF16)],
        compiler_params=_params("arbitrary", "arbitrary"),
        name="inproj",
    )(h, *([w_in] * INPROJ_WSPLIT), *mixer_weights)


def _lru_gate_logits(n, xr_ref, cw_ref, cb_ref, wa_ref, ba_ref, wx_ref, bx_ref, ext_ref):
    tc = xr_ref.shape[0]
    bw = wa_ref.shape[1]
    sl = slice(n * bw, (n + 1) * bw)
    ext_ref[SUBLANES:SUBLANES + tc, sl] = xr_ref[:, sl].astype(F32)
    kw = cw_ref.shape[0]
    u = cb_ref[:, sl] + cw_ref[kw - 1:kw, sl] * ext_ref[SUBLANES:SUBLANES + tc, sl]
    for k in range(kw - 1):
        back = kw - 1 - k
        u = u + cw_ref[k:k + 1, sl] * ext_ref[SUBLANES - back:SUBLANES - back + tc, sl]
    ext_ref[0:SUBLANES, sl] = ext_ref[tc:tc + SUBLANES, sl]
    ub = u.astype(BF16)
    r_logit = jnp.dot(ub, wa_ref[n], preferred_element_type=F32) + ba_ref[:, sl]
    i_logit = jnp.dot(ub, wx_ref[n], preferred_element_type=F32) + bx_ref[:, sl]
    return u, r_logit, i_logit


def _lru_scan(n, gates, gg_ref, lam_ref, h_ref, out_ref):
    u, r_logit, i_logit = gates
    tc, bw = u.shape
    sl = slice(n * bw, (n + 1) * bw)
    nlam = -lam_ref[:, sl]
    softplus = jnp.maximum(nlam, 0.0) + jnp.log1p(jnp.exp(-jnp.abs(nlam)))
    log_a = _sigmoid(r_logit) * (-LRU_C * softplus)
    a_all = jnp.exp(log_a)
    b_all = jnp.sqrt(-jnp.tanh(log_a) * (a_all * a_all + 1.0)) * (_sigmoid(i_logit) * u)

    row = lax.broadcasted_iota(jnp.int32, (SUBLANES, bw), 0)
    h = h_ref[:, sl]
    out = []
    for k in range(tc // SUBLANES):
        rows = slice(k * SUBLANES, (k + 1) * SUBLANES)
        a, b = a_all[rows, :], b_all[rows, :]
        for dist in (1, 2, 4):
            keep = row >= dist
            b = b + a * jnp.where(keep, pltpu.roll(b, dist, 0), 0.0)
            a = a * jnp.where(keep, pltpu.roll(a, dist, 0), 1.0)
        hs = a * h + b
        out.append(hs)
        h = hs[SUBLANES - 1:SUBLANES, :]
    h_ref[:, sl] = h
    out_ref[:, sl] = (gg_ref[:, sl].astype(F32) * jnp.concatenate(out, axis=0)).astype(BF16)


def _attn_kernel(sink_ref, q_ref, kvo_ref, kvp_ref, o_ref, *, layer):
    tq = q_ref.shape[0]
    first_valid_key = jnp.where(pl.program_id(1) == 0, WINDOW, 0)
    pairs = N_HEADS // N_KV_HEADS // 2
    rows = pairs * WINDOW
    lane = lax.broadcasted_iota(jnp.int32, (2 * WINDOW, LANES), 1)
    lo = lane < HEAD_DIM
    qpos = lax.broadcasted_iota(jnp.int32, (rows, 2 * WINDOW), 0) % WINDOW
    kpos = lax.broadcasted_iota(jnp.int32, (rows, 2 * WINDOW), 1)
    band = (kpos > qpos) & (kpos <= qpos + WINDOW)
    bias_rest = jnp.where(band, 0.0, MASK_VALUE)
    bias_first = jnp.where(band & (kpos >= first_valid_key), 0.0, MASK_VALUE)
    pair_of_row = lax.broadcasted_iota(jnp.int32, (rows, 1), 0) // WINDOW
    out_lo = lax.broadcasted_iota(jnp.int32, (rows, LANES), 1) < HEAD_DIM

    def sink_col(kvh, half):
        col = jnp.zeros((rows, 1), F32)
        for p in range(pairs):
            head = (N_HEADS // N_KV_HEADS) * kvh + 2 * p + half
            col = jnp.where(pair_of_row == p, sink_ref[layer, head], col)
        return col

    sink_cols = [[sink_col(kvh, half) for half in range(2)] for kvh in range(N_KV_HEADS)]

    for jb in range(tq // WINDOW):
        own = kvo_ref[jb * WINDOW:(jb + 1) * WINDOW, :]
        prev = kvp_ref[...] if jb == 0 else kvo_ref[(jb - 1) * WINDOW:jb * WINDOW, :]
        kv = jnp.concatenate([prev, own], axis=0).astype(F32)
        kk, vv = kv[:, :LANES], kv[:, LANES:]
        kk_rot = pltpu.roll(kk, HEAD_DIM, 1)
        vv_rot = pltpu.roll(vv, HEAD_DIM, 1)
        bias = bias_first if jb == 0 else bias_rest
        for kvh in range(N_KV_HEADS):
            if kvh == 0:
                k_top, k_bot, v_top, v_bot = kk, kk_rot, vv, vv_rot
            else:
                k_top, k_bot, v_top, v_bot = kk_rot, kk, vv_rot, vv
            kbd = jnp.concatenate([jnp.where(lo, k_top, 0.0), jnp.where(lo, 0.0, k_bot)], axis=0).astype(BF16)
            vbd = jnp.concatenate([jnp.where(lo, v_top, 0.0), jnp.where(lo, 0.0, v_bot)], axis=0).astype(BF16)
            qs = jnp.concatenate(
                [q_ref[jb * WINDOW:(jb + 1) * WINDOW, (pairs * kvh + p) * LANES:(pairs * kvh + p + 1) * LANES]
                 for p in range(pairs)], axis=0)
            qs = (qs.astype(F32) * SOFTMAX_SCALE).astype(BF16)
            s = lax.dot_general(qs, kbd, (((1,), (1,)), ((), ())), preferred_element_type=F32)
            probs, inv = [], []
            for half in range(2):
                sh = s[:, half * 2 * WINDOW:(half + 1) * 2 * WINDOW] + bias
                sink = sink_cols[kvh][half]
                m = jnp.maximum(jnp.max(sh, axis=-1, keepdims=True), sink)
                e = jnp.exp(sh - m)
                den = jnp.sum(e, axis=-1, keepdims=True) + jnp.exp(sink - m)
                probs.append(e.astype(BF16))
                inv.append(1.0 / den)
            pv = jnp.dot(jnp.concatenate(probs, axis=1), vbd, preferred_element_type=F32)
            pv = pv * jnp.where(out_lo, inv[0], inv[1])
            for p in range(pairs):
                col = (pairs * kvh + p) * LANES
                o_ref[jb * WINDOW:(jb + 1) * WINDOW, col:col + LANES] = (
                    pv[p * WINDOW:(p + 1) * WINDOW, :].astype(o_ref.dtype))


def _attention(z, sinks, layer, batch, seq, q_block, kv_block):
    tok = z.shape[0]
    width = N_HEADS * HEAD_DIM
    kvw = 2 * N_KV_HEADS * HEAD_DIM
    nt = seq // ATTN_TQ
    per = ATTN_TQ // WINDOW
    kern = functools.partial(_attn_kernel, layer=layer)
    return pl.pallas_call(
        kern,
        grid=(batch, nt),
        in_specs=[
            pl.BlockSpec(memory_space=pltpu.SMEM),
            pl.BlockSpec((ATTN_TQ, width), lambda b, i: (b * nt + i, q_block)),
            pl.BlockSpec((ATTN_TQ, kvw), lambda b, i: (b * nt + i, kv_block)),
            pl.BlockSpec((WINDOW, kvw), lambda b, i: (jnp.maximum((b * nt + i) * per - 1, 0), kv_block)),
        ],
        out_specs=pl.BlockSpec((ATTN_TQ, width), lambda b, i: (b * nt + i, 0)),
        out_shape=jax.ShapeDtypeStruct((tok, width), BF16),
        compiler_params=_params("arbitrary", "arbitrary"),
        name="swattn",
    )(sinks, z, z, z)


def _mixer_kernel(xr_ref, gg_ref, cw_ref, cb_ref, wla_ref, ba_ref, wlx_ref, bx_ref, lam_ref,
                  x_ref, attn_ref, sga_ref, sgb_ref, g2_ref, wr_ref, wa_ref, wo_ref, o_ref,
                  rnn_ref, rnp_ref, mrg_ref, mrp_ref, ext_ref, h_ref, *, n_tiles, tiles_per_seq):
    j = pl.program_id(0)
    d = o_ref.shape[1]
    nb, bw, _ = wla_ref.shape
    assert d // MIXER_TN == nb

    @pl.when(j == 0)
    def _():
        rnn_ref[...] = jnp.zeros_like(rnn_ref)
        mrg_ref[...] = jnp.zeros_like(mrg_ref)

    @pl.when(jnp.minimum(j, n_tiles - 1) % tiles_per_seq == 0)
    def _():
        ext_ref[0:SUBLANES, :] = jnp.zeros((SUBLANES, ext_ref.shape[1]), F32)
        h_ref[...] = jnp.zeros_like(h_ref)

    rnp_ref[...] = rnn_ref[...]
    mrp_ref[...] = mrg_ref[...]

    def gate_logits(n):
        return _lru_gate_logits(n, xr_ref, cw_ref, cb_ref, wla_ref, ba_ref, wlx_ref, bx_ref, ext_ref)

    for n in range(nb):
        cs = slice(n * MIXER_TN, (n + 1) * MIXER_TN)
        br = jnp.dot(rnp_ref[...], wr_ref[:, cs], preferred_element_type=F32)
        ba = jnp.dot(attn_ref[...], wa_ref[:, cs], preferred_element_type=F32)
        mrg_ref[:, cs] = (sga_ref[:, cs].astype(F32) * br + sgb_ref[:, cs].astype(F32) * ba).astype(BF16)
        proj = jnp.dot(mrp_ref[...], wo_ref[:, cs], preferred_element_type=F32)
        o_ref[:, cs] = x_ref[:, cs] + g2_ref[:, cs] * proj
        _lru_scan(n, gate_logits(n), gg_ref, lam_ref, h_ref, rnn_ref)


def _mixer(x, attn, z, mod_l, conv_w, conv_b, wla, ba, wlx, bx, lam, w_br_rnn, w_br_attn, w_out,
           layer, seq, xr_block, gr_block):
    tok, d = x.shape
    width = conv_w.shape[2]
    n_tiles = tok // MIXER_TM
    tps = seq // MIXER_TM
    resident = pl.Buffered(1)
    rec_tile = lambda j: jnp.minimum(j, n_tiles - 1)
    mrg_tile = lambda j: jnp.clip(j - 1, 0, n_tiles - 1)
    out_tile = lambda j: jnp.maximum(j - 2, 0)
    vec = lambda: pl.BlockSpec((None, 1, width), lambda j: (layer, 0, 0))
    blk = lambda: pl.BlockSpec((None,) + wla.shape[1:], lambda j: (layer, 0, 0, 0))
    kern = functools.partial(_mixer_kernel, n_tiles=n_tiles, tiles_per_seq=tps)
    return pl.pallas_call(
        kern,
        grid=(n_tiles + 2,),
        in_specs=[
            pl.BlockSpec((MIXER_TM, width), lambda j: (rec_tile(j), xr_block)),
            pl.BlockSpec((MIXER_TM, width), lambda j: (rec_tile(j), gr_block)),
            pl.BlockSpec((None,) + conv_w.shape[1:], lambda j: (layer, 0, 0)),
            vec(), blk(), vec(), blk(), vec(), vec(),
            pl.BlockSpec((MIXER_TM, d), lambda j: (out_tile(j), 0)),
            pl.BlockSpec((MIXER_TM, width), lambda j: (mrg_tile(j), 0)),
            pl.BlockSpec((MIXER_TM, d), lambda j: (mrg_tile(j), 0)),
            pl.BlockSpec((MIXER_TM, d), lambda j: (mrg_tile(j), 1)),
            _mod_spec(d, lambda j: out_tile(j) // tps, 5),
            pl.BlockSpec((width, d), lambda j: (0, 0), pipeline_mode=resident),
            pl.BlockSpec((width, d), lambda j: (0, 0), pipeline_mode=resident),
            pl.BlockSpec((d, d), lambda j: (0, 0), pipeline_mode=resident),
        ],
        out_specs=pl.BlockSpec((MIXER_TM, d), lambda j: (out_tile(j), 0)),
        out_shape=jax.ShapeDtypeStruct((tok, d), F32),
        scratch_shapes=[
            pltpu.VMEM((MIXER_TM, width), BF16),
            pltpu.VMEM((MIXER_TM, width), BF16),
            pltpu.VMEM((MIXER_TM, d), BF16),
            pltpu.VMEM((MIXER_TM, d), BF16),
            pltpu.VMEM((MIXER_TM + SUBLANES, width), F32),
            pltpu.VMEM((1, width), F32),
        ],
        compiler_params=_params("arbitrary"),
        name="mixer",
    )(z, z, conv_w, conv_b, wla, ba, wlx, bx, lam, x, attn, z, z, mod_l, w_br_rnn, w_br_attn, w_out)


def kernel(x, c, g_ffn1, w_ffn1_up, w_ffn1_down, g_mix, w_in, conv_w, conv_b, lru_wa, lru_ba, lru_wx,
           lru_bx, lru_lambda, attn_sinks, w_br_rnn, w_br_attn, w_out, g_ffn2, w_ffn2_up, w_ffn2_down,
           w_mod, b_mod, g_final):
    batch, seq, d = x.shape
    depth = w_mod.shape[0]
    lru_w = conv_w.shape[2]
    attn_w = N_HEADS * HEAD_DIM
    kv_w = N_KV_HEADS * HEAD_DIM

    gate_col0 = 2 * lru_w + attn_w + 2 * kv_w
    gates_w = 2 * d
    xr_block = gates_w // lru_w
    gr_block = xr_block + 1
    q_block = (gates_w + 2 * lru_w) // attn_w
    kv_block = (gates_w + 2 * lru_w + attn_w) // (2 * kv_w)
    wa_b, wx_b = lru_wa.astype(BF16), lru_wx.astype(BF16)

    vec3 = lambda a: a.reshape(depth, 1, a.shape[-1])
    g1, gm, g2 = vec3(g_ffn1), vec3(g_mix), vec3(g_ffn2)
    cb, ba, bx, lam = vec3(conv_b), vec3(lru_ba), vec3(lru_bx), vec3(lru_lambda)
    gf = g_final.reshape(1, d)

    mod_rows = 2 * SUBLANES
    c_pad = jnp.pad(c, ((0, mod_rows - batch), (0, 0)))
    mod, up_bf, dn_bf = _modulation(c_pad, w_mod, b_mod, w_ffn1_up, w_ffn1_down)

    xs = x.reshape(batch * seq, d)
    for l in range(depth):
        mod_l = mod[l].reshape(mod_rows, 1, N_MOD * d)
        xs, up_bf, dn_bf = _ffn(xs, g1, l, mod_l, 0, up_bf, dn_bf, seq, gf, False,
                                (w_ffn2_up, w_ffn2_down, l))
        h = _normmod_call(xs, gm, mod_l, 3, l, seq)
        z, wbr, wba, wo = _inproj(h, w_in, l, gate_col0, (w_br_rnn, w_br_attn, w_out))
        attn = _attention(z, attn_sinks, l, batch, seq, q_block, kv_block)
        xs = _mixer(xs, attn, z, mod_l, conv_w, cb, wa_b, ba, wx_b, bx, lam, wbr, wba, wo,
                    l, seq, xr_block, gr_block)
        if l + 1 < depth:
            xs, up_bf, dn_bf = _ffn(xs, g2, l, mod_l, 6, up_bf, dn_bf, seq, gf, False,
                                    (w_ffn1_up, w_ffn1_down, l + 1))
        else:
            xs, = _ffn(xs, g2, l, mod_l, 6, up_bf, dn_bf, seq, gf, True, None)
    return xs.reshape(batch, seq, d)
```

```python
import functools

import jax
import jax.numpy as jnp
from jax import lax
from jax.experimental import pallas as pl
from jax.experimental.pallas import tpu as pltpu

F32 = jnp.float32
BF16 = jnp.bfloat16

N_HEADS = 16
N_KV_HEADS = 2
HEAD_DIM = 64
WINDOW = 128
LRU_C = 8.0
FFN_RES = 0.5
EPS = 1e-6
N_MOD = 9
SOFTMAX_SCALE = HEAD_DIM ** -0.5
assert SOFTMAX_SCALE == 2.0 ** -3
MASK_VALUE = -1e30

VMEM_LIMIT_BYTES = 56 * 1024 * 1024
SUBLANES = 8
LANES = 128

MOD_TN = 1024
NORM_TM = 1024
NORM_ROWS = 32
FFN_TM = 1024
FFN_TF = 512
FFN_UP_SPLIT = 2
FFN_CAST_STEPS = 64
FFN_CAST_DOWN_ROWS = 128
FFN_NORM_PIECES = 4
FFN_XROWS = 128
FFN_DOWN_TN = 512
INPROJ_TM = 1024
INPROJ_TN = 1024
ATTN_TQ = 512
MIXER_TM = 256
MIXER_TN = 256


def _params(*sem):
    return pltpu.CompilerParams(dimension_semantics=sem, vmem_limit_bytes=VMEM_LIMIT_BYTES)


def _sigmoid(x):
    return 0.5 * jnp.tanh(0.5 * x) + 0.5


def _gelu_tanh(x):
    c = 0.7978845608028654
    return 0.5 * x * (1.0 + jnp.tanh(c * (x + 0.044715 * (x * x * x))))


def _rmsnorm(x, g):
    ms = jnp.mean(x * x, axis=-1, keepdims=True)
    return x * lax.rsqrt(ms + EPS) * g


def _normmod(x, g, shift, scale):
    return _rmsnorm(x, g) * (1.0 + scale) + shift


def _mod_kernel(c_ref, w_ref, b_ref, cui_ref, cdi_ref, o_ref, cuo_ref, cdo_ref):
    c = c_ref[...]
    ca = (c * _sigmoid(c)).astype(BF16)
    o_ref[...] = jnp.dot(ca, w_ref[...].astype(BF16), preferred_element_type=F32) + b_ref[...]

    @pl.when(pl.program_id(0) * pl.num_programs(1) + pl.program_id(1) < FFN_CAST_STEPS)
    def _():
        cuo_ref[...] = cui_ref[...].astype(BF16)
        cdo_ref[...] = cdi_ref[...].astype(BF16)


def _cast_specs(src_up, src_down, src_layer, step_of):
    _, d, two_dff = src_up.shape
    dff = src_down.shape[1]
    up_rows, down_rows = d // FFN_CAST_STEPS, FFN_CAST_DOWN_ROWS
    assert dff // down_rows <= FFN_CAST_STEPS
    up_blk = lambda *i: jnp.minimum(step_of(*i), FFN_CAST_STEPS - 1)
    down_blk = lambda *i: jnp.minimum(step_of(*i), dff // down_rows - 1)
    in_specs = [pl.BlockSpec((None, up_rows, two_dff), lambda *i: (src_layer, up_blk(*i), 0)),
                pl.BlockSpec((None, down_rows, d), lambda *i: (src_layer, down_blk(*i), 0))]
    out_specs = [pl.BlockSpec((up_rows, two_dff), lambda *i: (up_blk(*i), 0)),
                 pl.BlockSpec((down_rows, d), lambda *i: (down_blk(*i), 0))]
    out_shape = [jax.ShapeDtypeStruct((d, two_dff), BF16), jax.ShapeDtypeStruct((dff, d), BF16)]
    return in_specs, out_specs, out_shape


def _modulation(c_pad, w_mod, b_mod, w_up, w_down):
    depth, d, nd = w_mod.shape
    rows = c_pad.shape[0]
    nj = nd // MOD_TN
    assert depth * nj >= FFN_CAST_STEPS
    cast_in, cast_out, cast_shape = _cast_specs(w_up, w_down, 0, lambda l, j: l * nj + j)
    return pl.pallas_call(
        _mod_kernel,
        grid=(depth, nj),
        in_specs=[
            pl.BlockSpec((rows, d), lambda l, j: (0, 0)),
            pl.BlockSpec((None, d, MOD_TN), lambda l, j: (l, 0, j)),
            pl.BlockSpec((None, 1, MOD_TN), lambda l, j: (l, 0, j)),
        ] + cast_in,
        out_specs=[pl.BlockSpec((None, rows, MOD_TN), lambda l, j: (l, 0, j))] + cast_out,
        out_shape=[jax.ShapeDtypeStruct((depth, rows, nd), F32)] + cast_shape,
        compiler_params=_params("arbitrary", "arbitrary"),
        name="modulation",
    )(c_pad, w_mod, b_mod.reshape(depth, 1, nd), w_up, w_down)


def _mod_spec(d, batch_of, chunk):
    return pl.BlockSpec((None, 1, d), lambda *idx: (batch_of(*idx), 0, chunk))


def _normmod_kernel(x_ref, g_ref, sh_ref, sc_ref, o_ref):
    g, sh, sc = g_ref[...], sh_ref[...], sc_ref[...]

    def body(k, carry):
        rows = pl.ds(pl.multiple_of(k * NORM_ROWS, NORM_ROWS), NORM_ROWS)
        o_ref[rows, :] = _normmod(x_ref[rows, :], g, sh, sc).astype(o_ref.dtype)
        return carry

    lax.fori_loop(0, x_ref.shape[0] // NORM_ROWS, body, 0, unroll=4)


def _normmod_call(x, g, mod_l, chunk0, layer, seq):
    tok, d = x.shape
    tpb = seq // NORM_TM
    return pl.pallas_call(
        _normmod_kernel,
        grid=(tok // NORM_TM,),
        in_specs=[
            pl.BlockSpec((NORM_TM, d), lambda i: (i, 0)),
            pl.BlockSpec((None, 1, d), lambda i: (layer, 0, 0)),
            _mod_spec(d, lambda i: i // tpb, chunk0),
            _mod_spec(d, lambda i: i // tpb, chunk0 + 1),
        ],
        out_specs=pl.BlockSpec((NORM_TM, d), lambda i: (i, 0)),
        out_shape=jax.ShapeDtypeStruct((tok, d), BF16),
        compiler_params=_params("arbitrary"),
        name="normmod",
    )(x, g, mod_l, mod_l)


def _ffn_kernel(*refs, nf, n_tiles, final_norm, cast_next):
    (xa_ref, g_ref, sh_ref, sc_ref, x_ref, ga_ref, wg_ref, wu_ref, wd_ref, gf_ref), refs = refs[:10], refs[10:]
    if cast_next:
        (cui_ref, cdi_ref, o_ref, cuo_ref, cdo_ref), refs = refs[:5], refs[5:]
    else:
        o_ref, refs = refs[0], refs[1:]
    hn_ref, hc_ref, act_ref = refs
    s = pl.program_id(0)
    tm, d = o_ref.shape
    nx = tm // FFN_XROWS
    t = s - nx
    tu = jnp.clip(t, 0, n_tiles * nf - 1)
    iu, fu = tu // nf, tu % nf
    fd = jnp.maximum(t - 1, 0) % nf
    slot = s % 2

    if cast_next:
        @pl.when(s < FFN_CAST_STEPS)
        def _():
            cuo_ref[...] = cui_ref[...].astype(BF16)
            cdo_ref[...] = cdi_ref[...].astype(BF16)

    piece_rows = FFN_XROWS // FFN_NORM_PIECES
    pieces_per_half = FFN_NORM_PIECES // FFN_UP_SPLIT

    def norm_piece(p, rchunk):
        rs = slice(p * piece_rows, (p + 1) * piece_rows)
        r0 = pl.multiple_of(rchunk * FFN_XROWS + p * piece_rows, piece_rows)
        hn = _normmod(xa_ref[rs, :], g_ref[...], sh_ref[...], sc_ref[...])
        hn_ref[pl.ds(r0, piece_rows), :] = hn.astype(BF16)

    @pl.when(t < 0)
    def _():
        for p in range(FFN_NORM_PIECES):
            norm_piece(p, s)

    @pl.when(t >= 0)
    def _():
        @pl.when(t == 0)
        def _():
            act_ref[...] = jnp.zeros_like(act_ref)

        @pl.when(fu == 0)
        def _():
            hc_ref[...] = hn_ref[...]

        def step(first_chunk):
            ahead_chunk = jnp.minimum(fu, nx - 1)

            act_prev = act_ref[1 - slot]
            coef = FFN_RES * ga_ref[...]
            for n in range(d // FFN_DOWN_TN):
                cs = slice(n * FFN_DOWN_TN, (n + 1) * FFN_DOWN_TN)
                part = coef[:, cs] * jnp.dot(act_prev, wd_ref[:, cs], preferred_element_type=F32)
                o_ref[:, cs] = part if first_chunk else o_ref[:, cs] + part

            r0 = pl.multiple_of(jnp.minimum(fd, nx - 1) * FFN_XROWS, FFN_XROWS)
            take = jnp.where(fd < nx, 1.0, 0.0)
            o_ref[pl.ds(r0, FFN_XROWS), :] += take * x_ref[...]

            for m in range(FFN_UP_SPLIT):
                rs = slice(m * tm // FFN_UP_SPLIT, (m + 1) * tm // FFN_UP_SPLIT)
                hm = hc_ref[rs, :]
                gate = jnp.dot(hm, wg_ref[...], preferred_element_type=F32)
                for p in range(m * pieces_per_half, (m + 1) * pieces_per_half):
                    norm_piece(p, ahead_chunk)
                up = jnp.dot(hm, wu_ref[...], preferred_element_type=F32)
                act_ref[slot, rs, :] = (gate * _sigmoid(gate) * up).astype(BF16)

        @pl.when(fd == 0)
        def _():
            step(True)

        @pl.when(fd != 0)
        def _():
            step(False)

        if final_norm:
            @pl.when(fd == nf - 1)
            def _():
                o_ref[...] = _rmsnorm(o_ref[...], gf_ref[...])


def _ffn(x, g, g_layer, mod_l, chunk0, w_up, w_down, seq, g_final, final_norm, cast_src):
    tok, d = x.shape
    dff = w_down.shape[0]
    nf = dff // FFN_TF
    n_tiles = tok // FFN_TM
    n_chunks = n_tiles * nf
    n_steps = FFN_TM // FFN_XROWS + n_chunks + 1
    tpb = seq // FFN_TM
    nx = FFN_TM // FFN_XROWS
    up_chunk = lambda s: jnp.clip(s - nx, 0, n_chunks - 1)
    down_tile = lambda s: jnp.maximum(s - nx - 1, 0) // nf
    down_chunk = lambda s: jnp.maximum(s - nx - 1, 0) % nf
    ahead_tile = lambda s: jnp.where(s < nx, 0, jnp.minimum(up_chunk(s) // nf + 1, n_tiles - 1))
    ahead_rows = lambda s: jnp.where(s < nx, s, jnp.minimum(up_chunk(s) % nf, nx - 1))
    in_specs = [
        pl.BlockSpec((FFN_XROWS, d), lambda s: (ahead_tile(s) * nx + ahead_rows(s), 0)),
        pl.BlockSpec((None, 1, d), lambda s: (g_layer, 0, 0)),
        _mod_spec(d, lambda s: ahead_tile(s) // tpb, chunk0),
        _mod_spec(d, lambda s: ahead_tile(s) // tpb, chunk0 + 1),
        pl.BlockSpec((FFN_XROWS, d), lambda s: (down_tile(s) * nx + jnp.minimum(down_chunk(s), nx - 1), 0)),
        _mod_spec(d, lambda s: down_tile(s) // tpb, chunk0 + 2),
        pl.BlockSpec((d, FFN_TF), lambda s: (0, up_chunk(s) % nf)),
        pl.BlockSpec((d, FFN_TF), lambda s: (0, nf + up_chunk(s) % nf)),
        pl.BlockSpec((FFN_TF, d), lambda s: (down_chunk(s), 0)),
        pl.BlockSpec((1, d), lambda s: (0, 0)),
    ]
    args = [x, g, mod_l, mod_l, x, mod_l, w_up, w_up, w_down, g_final]
    out_specs = [pl.BlockSpec((FFN_TM, d), lambda s: (down_tile(s), 0))]
    out_shape = [jax.ShapeDtypeStruct((tok, d), F32)]
    if cast_src is not None:
        src_up, src_down, src_layer = cast_src
        assert FFN_CAST_STEPS <= n_steps
        cast_in, cast_out, cast_shape = _cast_specs(src_up, src_down, src_layer, lambda s: s)
        in_specs += cast_in
        args += [src_up, src_down]
        out_specs += cast_out
        out_shape += cast_shape
    kern = functools.partial(_ffn_kernel, nf=nf, n_tiles=n_tiles, final_norm=final_norm,
                             cast_next=cast_src is not None)
    return pl.pallas_call(
        kern,
        grid=(n_steps,),
        in_specs=in_specs,
        out_specs=out_specs,
        out_shape=out_shape,
        scratch_shapes=[pltpu.VMEM((FFN_TM, d), BF16), pltpu.VMEM((FFN_TM, d), BF16),
                        pltpu.VMEM((2, FFN_TM, FFN_TF), BF16)],
        compiler_params=_params("arbitrary"),
        name="ffn",
    )(*args)


INPROJ_SIGMOID_TILES = 4
INPROJ_GELU_TILE = 5
INPROJ_WSPLIT = 4
INPROJ_EPILOGUE_CHUNKS = 2


def _inproj_kernel(h_ref, *refs, n_full, tail):
    w_refs, refs = refs[:INPROJ_WSPLIT], refs[INPROJ_WSPLIT:]
    cast_in, o_ref, cast_out, wb_ref = refs[:3], refs[3], refs[4:7], refs[7]
    n = pl.program_id(0)
    sub = wb_ref.shape[1] // INPROJ_WSPLIT

    for src_ref, dst_ref in zip(cast_in, cast_out):
        dst_ref[...] = src_ref[...].astype(BF16)

    @pl.when(pl.program_id(1) == 0)
    def _():
        for k, w_ref in enumerate(w_refs):
            wb_ref[:, k * sub:(k + 1) * sub] = w_ref[...].astype(BF16)

    h = h_ref[...]

    def project(fn):
        cw = wb_ref.shape[1] // INPROJ_EPILOGUE_CHUNKS
        for k in range(INPROJ_EPILOGUE_CHUNKS):
            cs = slice(k * cw, (k + 1) * cw)
            o_ref[:, cs] = fn(jnp.dot(h, wb_ref[:, cs], preferred_element_type=F32)).astype(o_ref.dtype)

    @pl.when(n < INPROJ_SIGMOID_TILES)
    def _():
        project(_sigmoid)

    @pl.when(n == INPROJ_GELU_TILE)
    def _():
        project(_gelu_tanh)

    @pl.when((n >= INPROJ_SIGMOID_TILES) & (n != INPROJ_GELU_TILE) & (n < n_full))
    def _():
        project(lambda acc: acc)

    @pl.when(n == n_full)
    def _():
        o_ref[:, :tail] = jnp.dot(h, wb_ref[:, :tail], preferred_element_type=F32).astype(o_ref.dtype)


def _inproj(h, w_in, layer, gate_col0, mixer_weights):
    tok, d = h.shape
    ncol = w_in.shape[2]
    n_full = ncol // INPROJ_TN
    tail = ncol - n_full * INPROJ_TN
    sub = INPROJ_TN // INPROJ_WSPLIT
    gate_blk0 = gate_col0 // sub
    last_blk = ncol // sub - 1

    def w_spec(k):
        def index(n, i):
            base = jnp.where(n < INPROJ_SIGMOID_TILES, gate_blk0 + INPROJ_WSPLIT * n,
                             INPROJ_WSPLIT * (n - INPROJ_SIGMOID_TILES))
            return (layer, 0, jnp.minimum(base + k, last_blk))
        return pl.BlockSpec((None, d, sub), index)

    n_m = tok // INPROJ_TM
    n_steps = (n_full + 1) * n_m
    cast_in, cast_out, cast_shape = [], [], []
    for w in mixer_weights:
        rows = w.shape[1] // n_steps
        assert rows * n_steps == w.shape[1] and rows % (2 * SUBLANES) == 0
        cast_in.append(pl.BlockSpec((None, rows) + w.shape[2:], lambda n, i: (layer, n * n_m + i, 0)))
        cast_out.append(pl.BlockSpec((rows,) + w.shape[2:], lambda n, i: (n * n_m + i, 0)))
        cast_shape.append(jax.ShapeDtypeStruct(w.shape[1:], BF16))

    kern = functools.partial(_inproj_kernel, n_full=n_full, tail=tail)
    return pl.pallas_call(
        kern,
        grid=(n_full + 1, n_m),
        in_specs=([pl.BlockSpec((INPROJ_TM, d), lambda n, i: (i, 0))] + [w_spec(k) for k in range(INPROJ_WSPLIT)]
                  + cast_in),
        out_specs=[pl.BlockSpec((INPROJ_TM, INPROJ_TN), lambda n, i: (i, n))] + cast_out,
        out_shape=[jax.ShapeDtypeStruct((tok, ncol), BF16)] + cast_shape,
        scratch_shapes=[pltpu.VMEM((d, INPROJ_TN), BF16)],
        compiler_params=_params("arbitrary", "arbitrary"),
        name="inproj",
    )(h, *([w_in] * INPROJ_WSPLIT), *mixer_weights)


def _lru_gate_logits(n, xr_ref, cw_ref, cb_ref, wa_ref, ba_ref, wx_ref, bx_ref, ext_ref):
    tc = xr_ref.shape[0]
    bw = wa_ref.shape[1]
    sl = slice(n * bw, (n + 1) * bw)
    ext_ref[SUBLANES:SUBLANES + tc, sl] = xr_ref[:, sl].astype(F32)
    kw = cw_ref.shape[0]
    u = cb_ref[:, sl] + cw_ref[kw - 1:kw, sl] * ext_ref[SUBLANES:SUBLANES + tc, sl]
    for k in range(kw - 1):
        back = kw - 1 - k
        u = u + cw_ref[k:k + 1, sl] * ext_ref[SUBLANES - back:SUBLANES - back + tc, sl]
    ext_ref[0:SUBLANES, sl] = ext_ref[tc:tc + SUBLANES, sl]
    ub = u.astype(BF16)
    r_logit = jnp.dot(ub, wa_ref[n], preferred_element_type=F32) + ba_ref[:, sl]
    i_logit = jnp.dot(ub, wx_ref[n], preferred_element_type=F32) + bx_ref[:, sl]
    return u, r_logit, i_logit


def _lru_scan(n, gates, gg_ref, lam_ref, h_ref, out_ref):
    u, r_logit, i_logit = gates
    tc, bw = u.shape
    sl = slice(n * bw, (n + 1) * bw)
    nlam = -lam_ref[:, sl]
    softplus = jnp.maximum(nlam, 0.0) + jnp.log1p(jnp.exp(-jnp.abs(nlam)))
    log_a = _sigmoid(r_logit) * (-LRU_C * softplus)
    a_all = jnp.exp(log_a)
    b_all = jnp.sqrt(-jnp.tanh(log_a) * (a_all * a_all + 1.0)) * (_sigmoid(i_logit) * u)

    row = lax.broadcasted_iota(jnp.int32, (SUBLANES, bw), 0)
    h = h_ref[:, sl]
    out = []
    for k in range(tc // SUBLANES):
        rows = slice(k * SUBLANES, (k + 1) * SUBLANES)
        a, b = a_all[rows, :], b_all[rows, :]
        for dist in (1, 2, 4):
            keep = row >= dist
            b = b + a * jnp.where(keep, pltpu.roll(b, dist, 0), 0.0)
            a = a * jnp.where(keep, pltpu.roll(a, dist, 0), 1.0)
        hs = a * h + b
        out.append(hs)
        h = hs[SUBLANES - 1:SUBLANES, :]
    h_ref[:, sl] = h
    out_ref[:, sl] = (gg_ref[:, sl].astype(F32) * jnp.concatenate(out, axis=0)).astype(BF16)


def _attn_kernel(sink_ref, q_ref, kvo_ref, kvp_ref, o_ref, *, layer):
    tq = q_ref.shape[0]
    first_valid_key = jnp.where(pl.program_id(1) == 0, WINDOW, 0)
    pairs = N_HEADS // N_KV_HEADS // 2
    rows = pairs * WINDOW
    lane = lax.broadcasted_iota(jnp.int32, (2 * WINDOW, LANES), 1)
    lo = lane < HEAD_DIM
    qpos = lax.broadcasted_iota(jnp.int32, (rows, 2 * WINDOW), 0) % WINDOW
    kpos = lax.broadcasted_iota(jnp.int32, (rows, 2 * WINDOW), 1)
    band = (kpos > qpos) & (kpos <= qpos + WINDOW)
    bias_rest = jnp.where(band, 0.0, MASK_VALUE)
    bias_first = jnp.where(band & (kpos >= first_valid_key), 0.0, MASK_VALUE)
    pair_of_row = lax.broadcasted_iota(jnp.int32, (rows, 1), 0) // WINDOW
    out_lo = lax.broadcasted_iota(jnp.int32, (rows, LANES), 1) < HEAD_DIM

    def sink_col(kvh, half):
        col = jnp.zeros((rows, 1), F32)
        for p in range(pairs):
            head = (N_HEADS // N_KV_HEADS) * kvh + 2 * p + half
            col = jnp.where(pair_of_row == p, sink_ref[layer, head], col)
        return col

    sink_cols = [[sink_col(kvh, half) for half in range(2)] for kvh in range(N_KV_HEADS)]

    for jb in range(tq // WINDOW):
        own = kvo_ref[jb * WINDOW:(jb + 1) * WINDOW, :]
        prev = kvp_ref[...] if jb == 0 else kvo_ref[(jb - 1) * WINDOW:jb * WINDOW, :]
        kv = jnp.concatenate([prev, own], axis=0).astype(F32)
        kk, vv = kv[:, :LANES], kv[:, LANES:]
        kk_rot = pltpu.roll(kk, HEAD_DIM, 1)
        vv_rot = pltpu.roll(vv, HEAD_DIM, 1)
        bias = bias_first if jb == 0 else bias_rest
        for kvh in range(N_KV_HEADS):
            if kvh == 0:
                k_top, k_bot, v_top, v_bot = kk, kk_rot, vv, vv_rot
            else:
                k_top, k_bot, v_top, v_bot = kk_rot, kk, vv_rot, vv
            kbd = jnp.concatenate([jnp.where(lo, k_top, 0.0), jnp.where(lo, 0.0, k_bot)], axis=0).astype(BF16)
            vbd = jnp.concatenate([jnp.where(lo, v_top, 0.0), jnp.where(lo, 0.0, v_bot)], axis=0).astype(BF16)
            qs = jnp.concatenate(
                [q_ref[jb * WINDOW:(jb + 1) * WINDOW, (pairs * kvh + p) * LANES:(pairs * kvh + p + 1) * LANES]
                 for p in range(pairs)], axis=0)
            qs = (qs.astype(F32) * SOFTMAX_SCALE).astype(BF16)
            s = lax.dot_general(qs, kbd, (((1,), (1,)), ((), ())), preferred_element_type=F32)
            probs, inv = [], []
            for half in range(2):
                sh = s[:, half * 2 * WINDOW:(half + 1) * 2 * WINDOW] + bias
                sink = sink_cols[kvh][half]
                m = jnp.maximum(jnp.max(sh, axis=-1, keepdims=True), sink)
                e = jnp.exp(sh - m)
                den = jnp.sum(e, axis=-1, keepdims=True) + jnp.exp(sink - m)
                probs.append(e.astype(BF16))
                inv.append(1.0 / den)
            pv = jnp.dot(jnp.concatenate(probs, axis=1), vbd, preferred_element_type=F32)
            pv = pv * jnp.where(out_lo, inv[0], inv[1])
            for p in range(pairs):
                col = (pairs * kvh + p) * LANES
                o_ref[jb * WINDOW:(jb + 1) * WINDOW, col:col + LANES] = (
                    pv[p * WINDOW:(p + 1) * WINDOW, :].astype(o_ref.dtype))


def _attention(z, sinks, layer, batch, seq, q_block, kv_block):
    tok = z.shape[0]
    width = N_HEADS * HEAD_DIM
    kvw = 2 * N_KV_HEADS * HEAD_DIM
    nt = seq // ATTN_TQ
    per = ATTN_TQ // WINDOW
    kern = functools.partial(_attn_kernel, layer=layer)
    return pl.pallas_call(
        kern,
        grid=(batch, nt),
        in_specs=[
            pl.BlockSpec(memory_space=pltpu.SMEM),
            pl.BlockSpec((ATTN_TQ, width), lambda b, i: (b * nt + i, q_block)),
            pl.BlockSpec((ATTN_TQ, kvw), lambda b, i: (b * nt + i, kv_block)),
            pl.BlockSpec((WINDOW, kvw), lambda b, i: (jnp.maximum((b * nt + i) * per - 1, 0), kv_block)),
        ],
        out_specs=pl.BlockSpec((ATTN_TQ, width), lambda b, i: (b * nt + i, 0)),
        out_shape=jax.ShapeDtypeStruct((tok, width), BF16),
        compiler_params=_params("arbitrary", "arbitrary"),
        name="swattn",
    )(sinks, z, z, z)


def _mixer_kernel(xr_ref, gg_ref, cw_ref, cb_ref, wla_ref, ba_ref, wlx_ref, bx_ref, lam_ref,
                  x_ref, attn_ref, sga_ref, sgb_ref, g2_ref, wr_ref, wa_ref, wo_ref, o_ref,
                  rnn_ref, rnp_ref, mrg_ref, mrp_ref, ext_ref, h_ref, *, n_tiles, tiles_per_seq):
    j = pl.program_id(0)
    d = o_ref.shape[1]
    nb, bw, _ = wla_ref.shape
    assert d // MIXER_TN == nb

    @pl.when(j == 0)
    def _():
        rnn_ref[...] = jnp.zeros_like(rnn_ref)
        mrg_ref[...] = jnp.zeros_like(mrg_ref)

    @pl.when(jnp.minimum(j, n_tiles - 1) % tiles_per_seq == 0)
    def _():
        ext_ref[0:SUBLANES, :] = jnp.zeros((SUBLANES, ext_ref.shape[1]), F32)
        h_ref[...] = jnp.zeros_like(h_ref)

    rnp_ref[...] = rnn_ref[...]
    mrp_ref[...] = mrg_ref[...]

    def gate_logits(n):
        return _lru_gate_logits(n, xr_ref, cw_ref, cb_ref, wla_ref, ba_ref, wlx_ref, bx_ref, ext_ref)

    for n in range(nb):
        cs = slice(n * MIXER_TN, (n + 1) * MIXER_TN)
        br = jnp.dot(rnp_ref[...], wr_ref[:, cs], preferred_element_type=F32)
        ba = jnp.dot(attn_ref[...], wa_ref[:, cs], preferred_element_type=F32)
        mrg_ref[:, cs] = (sga_ref[:, cs].astype(F32) * br + sgb_ref[:, cs].astype(F32) * ba).astype(BF16)
        proj = jnp.dot(mrp_ref[...], wo_ref[:, cs], preferred_element_type=F32)
        o_ref[:, cs] = x_ref[:, cs] + g2_ref[:, cs] * proj
        _lru_scan(n, gate_logits(n), gg_ref, lam_ref, h_ref, rnn_ref)


def _mixer(x, attn, z, mod_l, conv_w, conv_b, wla, ba, wlx, bx, lam, w_br_rnn, w_br_attn, w_out,
           layer, seq, xr_block, gr_block):
    tok, d = x.shape
    width = conv_w.shape[2]
    n_tiles = tok // MIXER_TM
    tps = seq // MIXER_TM
    resident = pl.Buffered(1)
    rec_tile = lambda j: jnp.minimum(j, n_tiles - 1)
    mrg_tile = lambda j: jnp.clip(j - 1, 0, n_tiles - 1)
    out_tile = lambda j: jnp.maximum(j - 2, 0)
    vec = lambda: pl.BlockSpec((None, 1, width), lambda j: (layer, 0, 0))
    blk = lambda: pl.BlockSpec((None,) + wla.shape[1:], lambda j: (layer, 0, 0, 0))
    kern = functools.partial(_mixer_kernel, n_tiles=n_tiles, tiles_per_seq=tps)
    return pl.pallas_call(
        kern,
        grid=(n_tiles + 2,),
        in_specs=[
            pl.BlockSpec((MIXER_TM, width), lambda j: (rec_tile(j), xr_block)),
            pl.BlockSpec((MIXER_TM, width), lambda j: (rec_tile(j), gr_block)),
            pl.BlockSpec((None,) + conv_w.shape[1:], lambda j: (layer, 0, 0)),
            vec(), blk(), vec(), blk(), vec(), vec(),
            pl.BlockSpec((MIXER_TM, d), lambda j: (out_tile(j), 0)),
            pl.BlockSpec((MIXER_TM, width), lambda j: (mrg_tile(j), 0)),
            pl.BlockSpec((MIXER_TM, d), lambda j: (mrg_tile(j), 0)),
            pl.BlockSpec((MIXER_TM, d), lambda j: (mrg_tile(j), 1)),
            _mod_spec(d, lambda j: out_tile(j) // tps, 5),
            pl.BlockSpec((width, d), lambda j: (0, 0), pipeline_mode=resident),
            pl.BlockSpec((width, d), lambda j: (0, 0), pipeline_mode=resident),
            pl.BlockSpec((d, d), lambda j: (0, 0), pipeline_mode=resident),
        ],
        out_specs=pl.BlockSpec((MIXER_TM, d), lambda j: (out_tile(j), 0)),
        out_shape=jax.ShapeDtypeStruct((tok, d), F32),
        scratch_shapes=[
            pltpu.VMEM((MIXER_TM, width), BF16),
            pltpu.VMEM((MIXER_TM, width), BF16),
            pltpu.VMEM((MIXER_TM, d), BF16),
            pltpu.VMEM((MIXER_TM, d), BF16),
            pltpu.VMEM((MIXER_TM + SUBLANES, width), F32),
            pltpu.VMEM((1, width), F32),
        ],
        compiler_params=_params("arbitrary"),
        name="mixer",
    )(z, z, conv_w, conv_b, wla, ba, wlx, bx, lam, x, attn, z, z, mod_l, w_br_rnn, w_br_attn, w_out)


def kernel(x, c, g_ffn1, w_ffn1_up, w_ffn1_down, g_mix, w_in, conv_w, conv_b, lru_wa, lru_ba, lru_wx,
           lru_bx, lru_lambda, attn_sinks, w_br_rnn, w_br_attn, w_out, g_ffn2, w_ffn2_up, w_ffn2_down,
           w_mod, b_mod, g_final):
    batch, seq, d = x.shape
    depth = w_mod.shape[0]
    lru_w = conv_w.shape[2]
    attn_w = N_HEADS * HEAD_DIM
    kv_w = N_KV_HEADS * HEAD_DIM

    gate_col0 = 2 * lru_w + attn_w + 2 * kv_w
    gates_w = 2 * d
    xr_block = gates_w // lru_w
    gr_block = xr_block + 1
    q_block = (gates_w + 2 * lru_w) // attn_w
    kv_block = (gates_w + 2 * lru_w + attn_w) // (2 * kv_w)
    wa_b, wx_b = lru_wa.astype(BF16), lru_wx.astype(BF16)

    vec3 = lambda a: a.reshape(depth, 1, a.shape[-1])
    g1, gm, g2 = vec3(g_ffn1), vec3(g_mix), vec3(g_ffn2)
    cb, ba, bx, lam = vec3(conv_b), vec3(lru_ba), vec3(lru_bx), vec3(lru_lambda)
    gf = g_final.reshape(1, d)

    mod_rows = 2 * SUBLANES
    c_pad = jnp.pad(c, ((0, mod_rows - batch), (0, 0)))
    mod, up_bf, dn_bf = _modulation(c_pad, w_mod, b_mod, w_ffn1_up, w_ffn1_down)

    xs = x.reshape(batch * seq, d)
    for l in range(depth):
        mod_l = mod[l].reshape(mod_rows, 1, N_MOD * d)
        xs, up_bf, dn_bf = _ffn(xs, g1, l, mod_l, 0, up_bf, dn_bf, seq, gf, False,
                                (w_ffn2_up, w_ffn2_down, l))
        h = _normmod_call(xs, gm, mod_l, 3, l, seq)
        z, wbr, wba, wo = _inproj(h, w_in, l, gate_col0, (w_br_rnn, w_br_attn, w_out))
        attn = _attention(z, attn_sinks, l, batch, seq, q_block, kv_block)
        xs = _mixer(xs, attn, z, mod_l, conv_w, cb, wa_b, ba, wx_b, bx, lam, wbr, wba, wo,
                    l, seq, xr_block, gr_block)
        if l + 1 < depth:
            xs, up_bf, dn_bf = _ffn(xs, g2, l, mod_l, 6, up_bf, dn_bf, seq, gf, False,
                                    (w_ffn1_up, w_ffn1_down, l + 1))
        else:
            xs, = _ffn(xs, g2, l, mod_l, 6, up_bf, dn_bf, seq, gf, True, None)
    return xs.reshape(batch, seq, d)
```
